```python
import math
import jax, jax.numpy as jnp
from jax import lax
import numpy as np

D_MODEL = 4096
BATCH = 1
SEQ = 8192
DEPTH = 2

GRID_W = 64
CTX_LEN = 256
HEAD_DIM = 128
GQA_HEADS = D_MODEL // 256
GQA_KV_HEADS = GQA_HEADS // 4
DIFF_HEADS = D_MODEL // 512
DIFF_QK_DIM = HEAD_DIM
DIFF_V_DIM = 2 * HEAD_DIM
GQA_Q_W = GQA_HEADS * HEAD_DIM
GQA_KV_W = GQA_KV_HEADS * HEAD_DIM
DIFF_QK_W = DIFF_HEADS * 2 * DIFF_QK_DIM
DIFF_V_W = DIFF_HEADS * DIFF_V_DIM
ATTN_SPLITS = (GQA_Q_W, GQA_Q_W + GQA_KV_W, GQA_Q_W + 2 * GQA_KV_W,
               GQA_Q_W + 2 * GQA_KV_W + DIFF_QK_W, GQA_Q_W + 2 * GQA_KV_W + 2 * DIFF_QK_W)
ATTN_IN_W = ATTN_SPLITS[-1] + DIFF_V_W
ATTN_OUT_W = GQA_Q_W + DIFF_V_W
Q_BLOCK = 128
ROPE_THETA = 10000.0
D_FF = 256 * math.ceil(8 * D_MODEL / 3 / 256)
D_RNN = D_MODEL
LRU_BLOCKS = D_MODEL // 256
LRU_BLOCK_W = D_RNN // LRU_BLOCKS
CONV_W = 4
CONV_LEFT = CONV_W // 2
RG_C = 8.0
N_EXPERTS = 8
TOP_K = 2
EXPERT_FF = D_MODEL
N_ATTN_LAYERS = (DEPTH + 1) // 2
N_REC_LAYERS = DEPTH // 2
EPS = 1e-6
F32 = jnp.float32

kernel_name = 'hybrid_diffusion_gqa_diffattn_rglru_moe'


def rms_norm(x, g):
    xf = x.astype(F32)
    y = xf * lax.rsqrt(jnp.mean(xf * xf, axis=-1, keepdims=True) + EPS)
    return (y * g.astype(F32)).astype(x.dtype)


def modulate(x, g, shift, scale):
    return rms_norm(x, g) * (1.0 + scale) + shift


def adaln(cvec, w_mod, b_mod):
    m = (jax.nn.silu(cvec) @ w_mod + b_mod).reshape(-1, 1, 6, D_MODEL)
    return tuple(m[:, :, j] for j in range(6))


def axial_rope(rows, head_dim):
    half = head_dim // 2
    inv = ROPE_THETA ** (-jnp.arange(0, half, 2, dtype=F32) / half)
    row = jnp.repeat(jnp.arange(rows, dtype=F32), GRID_W)
    col = jnp.tile(jnp.arange(GRID_W, dtype=F32), rows)
    ang = jnp.concatenate([row[:, None] * inv, col[:, None] * inv], axis=-1)
    return jnp.cos(ang), jnp.sin(ang)


def apply_rope(x, cos, sin):
    shape = (1, x.shape[1]) + (1,) * (x.ndim - 3) + (cos.shape[-1],)
    c = cos.reshape(shape)
    s = sin.reshape(shape)
    xf = x.astype(F32)
    x1, x2 = xf[..., 0::2], xf[..., 1::2]
    return jnp.stack([x1 * c - x2 * s, x1 * s + x2 * c], axis=-1).reshape(x.shape).astype(x.dtype)


def sweep_query_blocks(fn, q, *kv):
    b, s = q.shape[:2]
    nb = s // Q_BLOCK
    qb = jnp.moveaxis(q.reshape((b, nb, Q_BLOCK) + q.shape[2:]), 1, 0)
    out = lax.map(lambda blk: fn(blk, *kv), qb)
    return jnp.moveaxis(out, 0, 1).reshape((b, s) + out.shape[3:])


def gqa_attend(q, k, v):
    b, t, h, d = q.shape
    hkv = k.shape[2]
    qg = q.reshape(b, t, hkv, h // hkv, d)
    s = jnp.einsum('bqhgd,bkhd->bhgqk', qg, k).astype(F32) * d ** -0.5
    p = jax.nn.softmax(s, axis=-1).astype(v.dtype)
    return jnp.einsum('bhgqk,bkhd->bqhgd', p, v).reshape(b, t, h, d)


def diff_attend(q, k, v, lam):
    d = q.shape[-1]
    s = jnp.einsum('bqhmd,bkhmd->bhmqk', q, k).astype(F32) * d ** -0.5
    p = jax.nn.softmax(s, axis=-1)
    a = p[:, :, 0] - lam * p[:, :, 1]
    return jnp.einsum('bhqk,bkhe->bqhe', a.astype(v.dtype), v)


def swiglu(h, w_gate, w_up, w_down):
    return (jax.nn.silu(h @ w_gate) * (h @ w_up)) @ w_down


def _attn_heads(h, w_in, gqa_qn, gqa_kn, diff_qn, diff_kn):
    b, t, _ = h.shape
    q_a, k_a, v_a, q_b, k_b, v_b = jnp.split(h @ w_in, ATTN_SPLITS, axis=-1)
    q_a = rms_norm(q_a.reshape(b, t, GQA_HEADS, HEAD_DIM), gqa_qn)
    k_a = rms_norm(k_a.reshape(b, t, GQA_KV_HEADS, HEAD_DIM), gqa_kn)
    v_a = v_a.reshape(b, t, GQA_KV_HEADS, HEAD_DIM)
    q_b = rms_norm(q_b.reshape(b, t, DIFF_HEADS, 2, DIFF_QK_DIM), diff_qn)
    k_b = rms_norm(k_b.reshape(b, t, DIFF_HEADS, 2, DIFF_QK_DIM), diff_kn)
    v_b = v_b.reshape(b, t, DIFF_HEADS, DIFF_V_DIM)
    return q_a, k_a, v_a, q_b, k_b, v_b


def _attn_merge(o_a, o_b, subln, lam_init, w_out):
    b, t = o_a.shape[:2]
    o_b = rms_norm(o_b, subln) * (1.0 - lam_init)
    return jnp.concatenate([o_a.reshape(b, t, GQA_Q_W), o_b.reshape(b, t, DIFF_V_W)], axis=-1) @ w_out


def attention_mixer(h_lat, h_ctx, cos, sin, lam_init, need_ctx, w_in, gqa_qn, gqa_kn, diff_qn, diff_kn,
                    lam_q1, lam_k1, lam_q2, lam_k2, subln, w_out):
    lam = (jnp.exp(jnp.sum(lam_q1.astype(F32) * lam_k1.astype(F32)))
           - jnp.exp(jnp.sum(lam_q2.astype(F32) * lam_k2.astype(F32))) + lam_init)
    qa_l, ka_l, va_l, qb_l, kb_l, vb_l = _attn_heads(h_lat, w_in, gqa_qn, gqa_kn, diff_qn, diff_kn)
    qa_c, ka_c, va_c, qb_c, kb_c, vb_c = _attn_heads(h_ctx, w_in, gqa_qn, gqa_kn, diff_qn, diff_kn)
    qa_l = apply_rope(qa_l, cos, sin)
    ka_l = apply_rope(ka_l, cos, sin)
    qb_l = apply_rope(qb_l, cos, sin)
    kb_l = apply_rope(kb_l, cos, sin)
    ka = jnp.concatenate([ka_c, ka_l], axis=1)
    va = jnp.concatenate([va_c, va_l], axis=1)
    kb = jnp.concatenate([kb_c, kb_l], axis=1)
    vb = jnp.concatenate([vb_c, vb_l], axis=1)
    o_a = sweep_query_blocks(gqa_attend, qa_l, ka, va)
    o_b = sweep_query_blocks(lambda q, k, v: diff_attend(q, k, v, lam), qb_l, kb, vb)
    out_lat = _attn_merge(o_a, o_b, subln, lam_init, w_out)
    if not need_ctx:
        return out_lat, None
    out_ctx = _attn_merge(gqa_attend(qa_c, ka_c, va_c), diff_attend(qb_c, kb_c, vb_c, lam), subln, lam_init, w_out)
    return out_lat, out_ctx


def depthwise_conv_centred(x, w, b):
    t = x.shape[1]
    xp = jnp.pad(x, ((0, 0), (CONV_LEFT, CONV_W - 1 - CONV_LEFT), (0, 0)))
    y = xp[:, 0:t] * w[0]
    for k in range(1, CONV_W):
        y = y + xp[:, k:k + t] * w[k]
    return y + b


def block_diag_linear(x, w, b):
    xb = x.reshape(x.shape[:-1] + (LRU_BLOCKS, LRU_BLOCK_W))
    return jnp.einsum('btnd,nde->btne', xb, w).reshape(x.shape) + b


def _combine(e1, e2):
    a1, b1 = e1
    a2, b2 = e2
    return a1 * a2, a2 * b1 + b2


def linear_scan(a, b, h0, reverse):
    if reverse:
        a = jnp.flip(a, axis=1)
        b = jnp.flip(b, axis=1)
    b = b.at[:, 0].add(a[:, 0] * h0)
    _, h = lax.associative_scan(_combine, (a, b), axis=1)
    if reverse:
        h = jnp.flip(h, axis=1)
    return h


def rglru_scan(xc, w_a, b_a, w_x, b_x, lam, h0, reverse):
    r = jax.nn.sigmoid(block_diag_linear(xc, w_a, b_a).astype(F32))
    i = jax.nn.sigmoid(block_diag_linear(xc, w_x, b_x).astype(F32))
    log_a = -RG_C * r * jax.nn.softplus(-lam.astype(F32))
    a = jnp.exp(log_a)
    b = jnp.sqrt(-jnp.expm1(2.0 * log_a)) * (i * xc.astype(F32))
    return linear_scan(a, b, h0, reverse)


def recurrent_mixer(h_lat, h_ctx, need_ctx, w_in, conv_w, conv_b, ga_w, ga_b, gx_w, gx_b, lru_lam, w_out):
    y_lat, xr_lat = jnp.split(h_lat @ w_in, 2, axis=-1)
    xc_lat = depthwise_conv_centred(xr_lat, conv_w, conv_b)
    if need_ctx:
        y_ctx, xr_ctx = jnp.split(h_ctx @ w_in, 2, axis=-1)
    else:
        xr_ctx = h_ctx @ w_in[:, D_RNN:]
    xc_ctx = depthwise_conv_centred(xr_ctx, conv_w, conv_b)
    h0 = jnp.zeros((h_lat.shape[0], D_RNN), F32)
    s_lat = jnp.zeros(xc_lat.shape, F32)
    s_ctx = jnp.zeros(xc_ctx.shape, F32)
    for d, reverse in enumerate((False, True)):
        p = (ga_w[d], ga_b[d], gx_w[d], gx_b[d], lru_lam[d])
        hc = rglru_scan(xc_ctx, *p, h0, reverse)
        h_end = hc[:, 0] if reverse else hc[:, -1]
        s_lat = s_lat + rglru_scan(xc_lat, *p, h_end, reverse)
        if need_ctx:
            s_ctx = s_ctx + hc
    out_lat = (s_lat.astype(h_lat.dtype) * jax.nn.gelu(y_lat, approximate=True)) @ w_out
    if not need_ctx:
        return out_lat, None
    out_ctx = (s_ctx.astype(h_ctx.dtype) * jax.nn.gelu(y_ctx, approximate=True)) @ w_out
    return out_lat, out_ctx


def moe_swiglu(h, router_w, router_b, w_gate, w_up, w_down):
    logits = (h @ router_w).astype(F32) + router_b.astype(F32)
    top_v, top_i = lax.top_k(logits, TOP_K)
    top_w = jax.nn.softmax(top_v, axis=-1)
    comb = jnp.einsum('btk,btke->bte', top_w, jax.nn.one_hot(top_i, N_EXPERTS, dtype=F32)).astype(h.dtype)
    out = jnp.zeros_like(h)
    for e in range(N_EXPERTS):
        out = out + comb[..., e:e + 1] * swiglu(h, w_gate[e], w_up[e], w_down[e])
    return out


def setup_inputs(seed: int = 0) -> dict:
    key = jax.random.key(seed)
    keys = iter(jax.random.split(key, 64))
    na, nr, d = N_ATTN_LAYERS, N_REC_LAYERS, D_MODEL

    def normal(shape, scale):
        return scale * jax.random.normal(next(keys), shape, F32)

    def gain(shape):
        return 1.0 + 0.05 * jax.random.normal(next(keys), shape, F32)

    u = jax.random.uniform(next(keys), (nr, 2, D_RNN), F32, 0.9, 0.999)
    s = u ** (1.0 / RG_C)
    return {
        'x': normal((BATCH, SEQ, d), 1.0),
        'c': normal((BATCH, d), 1.0),
        'ctx': normal((BATCH, CTX_LEN, d), 1.0),
        'c_ctx': normal((d,), 1.0),
        'attn_w_mod': normal((na, d, 6 * d), d ** -0.5),
        'attn_b_mod': normal((na, 6 * d), 0.02),
        'attn_norm_mix': gain((na, d)),
        'attn_norm_ffn': gain((na, d)),
        'attn_w_in': normal((na, d, ATTN_IN_W), d ** -0.5),
        'attn_gqa_q_norm': gain((na, HEAD_DIM)),
        'attn_gqa_k_norm': gain((na, HEAD_DIM)),
        'attn_diff_q_norm': gain((na, 2, DIFF_QK_DIM)),
        'attn_diff_k_norm': gain((na, 2, DIFF_QK_DIM)),
        'attn_diff_lambda_q1': normal((na, DIFF_QK_DIM), 0.1),
        'attn_diff_lambda_k1': normal((na, DIFF_QK_DIM), 0.1),
        'attn_diff_lambda_q2': normal((na, DIFF_QK_DIM), 0.1),
        'attn_diff_lambda_k2': normal((na, DIFF_QK_DIM), 0.1),
        'attn_diff_subln': gain((na, DIFF_V_DIM)),
        'attn_w_out': normal((na, ATTN_OUT_W, d), ATTN_OUT_W ** -0.5),
        'ffn_w_gate': normal((na, d, D_FF), d ** -0.5),
        'ffn_w_up': normal((na, d, D_FF), d ** -0.5),
        'ffn_w_down': normal((na, D_FF, d), D_FF ** -0.5),
        'rec_w_mod': normal((nr, d, 6 * d), d ** -0.5),
        'rec_b_mod': normal((nr, 6 * d), 0.02),
        'rec_norm_mix': gain((nr, d)),
        'rec_norm_ffn': gain((nr, d)),
        'rec_w_in': normal((nr, d, 2 * D_RNN), d ** -0.5),
        'rec_conv_w': normal((nr, CONV_W, D_RNN), CONV_W ** -0.5),
        'rec_conv_b': normal((nr, D_RNN), 0.02),
        'rec_gate_a_w': normal((nr, 2, LRU_BLOCKS, LRU_BLOCK_W, LRU_BLOCK_W), LRU_BLOCK_W ** -0.5),
        'rec_gate_a_b': normal((nr, 2, D_RNN), 0.02),
        'rec_gate_x_w': normal((nr, 2, LRU_BLOCKS, LRU_BLOCK_W, LRU_BLOCK_W), LRU_BLOCK_W ** -0.5),
        'rec_gate_x_b': normal((nr, 2, D_RNN), 0.02),
        'rec_lru_lambda': jnp.log(s) - jnp.log1p(-s),
        'rec_w_out': normal((nr, D_RNN, d), D_RNN ** -0.5),
        'moe_router_w': normal((nr, d, N_EXPERTS), d ** -0.5),
        'moe_router_b': normal((nr, N_EXPERTS), 0.01),
        'moe_w_gate': normal((nr, N_EXPERTS, d, EXPERT_FF), d ** -0.5),
        'moe_w_up': normal((nr, N_EXPERTS, d, EXPERT_FF), d ** -0.5),
        'moe_w_down': normal((nr, N_EXPERTS, EXPERT_FF, d), EXPERT_FF ** -0.5),
    }


def reference(x, c, ctx, c_ctx,
              attn_w_mod, attn_b_mod, attn_norm_mix, attn_norm_ffn, attn_w_in,
              attn_gqa_q_norm, attn_gqa_k_norm, attn_diff_q_norm, attn_diff_k_norm,
              attn_diff_lambda_q1, attn_diff_lambda_k1, attn_diff_lambda_q2, attn_diff_lambda_k2,
              attn_diff_subln, attn_w_out, ffn_w_gate, ffn_w_up, ffn_w_down,
              rec_w_mod, rec_b_mod, rec_norm_mix, rec_norm_ffn, rec_w_in, rec_conv_w, rec_conv_b,
              rec_gate_a_w, rec_gate_a_b, rec_gate_x_w, rec_gate_x_b, rec_lru_lambda, rec_w_out,
              moe_router_w, moe_router_b, moe_w_gate, moe_w_up, moe_w_down):
    rows = x.shape[1] // GRID_W
    cos, sin = axial_rope(rows, HEAD_DIM)
    for layer in range(DEPTH):
        last = layer == DEPTH - 1
        i = layer // 2
        if layer % 2 == 0:
            l_sh1, l_sc1, l_g1, l_sh2, l_sc2, l_g2 = adaln(c, attn_w_mod[i], attn_b_mod[i])
            c_sh1, c_sc1, c_g1, c_sh2, c_sc2, c_g2 = adaln(c_ctx, attn_w_mod[i], attn_b_mod[i])
            lam_init = 0.8 - 0.6 * math.exp(-0.3 * layer)
            mix_lat, mix_ctx = attention_mixer(
                modulate(x, attn_norm_mix[i], l_sh1, l_sc1),
                modulate(ctx, attn_norm_mix[i], c_sh1, c_sc1),
                cos, sin, lam_init, not last, attn_w_in[i],
                attn_gqa_q_norm[i], attn_gqa_k_norm[i], attn_diff_q_norm[i], attn_diff_k_norm[i],
                attn_diff_lambda_q1[i], attn_diff_lambda_k1[i], attn_diff_lambda_q2[i], attn_diff_lambda_k2[i],
                attn_diff_subln[i], attn_w_out[i])
            x = x + l_g1 * mix_lat
            x = x + l_g2 * swiglu(modulate(x, attn_norm_ffn[i], l_sh2, l_sc2), ffn_w_gate[i], ffn_w_up[i], ffn_w_down[i])
            if not last:
                ctx = ctx + c_g1 * mix_ctx
                ctx = ctx + c_g2 * swiglu(modulate(ctx, attn_norm_ffn[i], c_sh2, c_sc2), ffn_w_gate[i], ffn_w_up[i], ffn_w_down[i])
        else:
            l_sh1, l_sc1, l_g1, l_sh2, l_sc2, l_g2 = adaln(c, rec_w_mod[i], rec_b_mod[i])
            c_sh1, c_sc1, c_g1, c_sh2, c_sc2, c_g2 = adaln(c_ctx, rec_w_mod[i], rec_b_mod[i])
            mix_lat, mix_ctx = recurrent_mixer(
                modulate(x, rec_norm_mix[i], l_sh1, l_sc1),
                modulate(ctx, rec_norm_mix[i], c_sh1, c_sc1), not last,
                rec_w_in[i], rec_conv_w[i], rec_conv_b[i], rec_gate_a_w[i], rec_gate_a_b[i],
                rec_gate_x_w[i], rec_gate_x_b[i], rec_lru_lambda[i], rec_w_out[i])
            x = x + l_g1 * mix_lat
            x = x + l_g2 * moe_swiglu(modulate(x, rec_norm_ffn[i], l_sh2, l_sc2), moe_router_w[i], moe_router_b[i],
                                      moe_w_gate[i], moe_w_up[i], moe_w_down[i])
            if not last:
                ctx = ctx + c_g1 * mix_ctx
                ctx = ctx + c_g2 * moe_swiglu(modulate(ctx, rec_norm_ffn[i], c_sh2, c_sc2), moe_router_w[i], moe_router_b[i],
                                              moe_w_gate[i], moe_w_up[i], moe_w_down[i])
    return x
```

```python
import functools
import math

import jax
import jax.numpy as jnp
from jax import lax
from jax.experimental import pallas as pl
from jax.experimental.pallas import tpu as pltpu

F32 = jnp.float32
BF16 = jnp.bfloat16
EPS = 1e-6
ROPE_THETA = 10000.0
GRID_W = 64
HEAD_DIM = 128
RG_C = 8.0
TOP_K = 2
LANES = 128
VMEM_LIMIT = 56 * 1024 * 1024
NT_DIMS = (((1,), (1,)), ((), ()))


def _cparams(sem, vmem=VMEM_LIMIT):
    return pltpu.CompilerParams(dimension_semantics=sem, vmem_limit_bytes=vmem)


def _tile(n, target, mult):
    best = None
    for t in range(mult, min(n, target) + 1, mult):
        if n % t == 0:
            best = t
    assert best is not None, (n, target, mult)
    return best


def _row_select(row0, tm, n_lat, ref):
    row = row0 + lax.broadcasted_iota(jnp.int32, (tm, 1), 0)
    return jnp.where(row < n_lat, ref[0:1, :], ref[1:2, :])


def _modvec_kernel(c_ref, w_ref, b_ref, o_ref):
    c = c_ref[...]
    a = (c * jax.nn.sigmoid(c)).astype(BF16)
    o_ref[...] = jnp.dot(a, w_ref[...].astype(BF16), preferred_element_type=F32) + b_ref[...]


def _modvec(c2, w_mod, b_mod):
    d, n = w_mod.shape
    tn = _tile(n, 512, LANES)
    return pl.pallas_call(
        _modvec_kernel,
        grid=(n // tn,),
        in_specs=[pl.BlockSpec((8, d), lambda j: (0, 0)),
                  pl.BlockSpec((d, tn), lambda j: (0, j)),
                  pl.BlockSpec((1, tn), lambda j: (0, j))],
        out_specs=pl.BlockSpec((8, tn), lambda j: (0, j)),
        out_shape=jax.ShapeDtypeStruct((8, n), F32),
        compiler_params=_cparams(("arbitrary",)),
        name="modvec",
    )(c2, w_mod, b_mod.reshape(1, n))


def _normmod_kernel(x_ref, g_ref, sh_ref, sc_ref, o_ref, *, tm, n_lat):
    x = x_ref[...]
    y = x * lax.rsqrt(jnp.mean(x * x, axis=-1, keepdims=True) + EPS) * g_ref[...]
    row0 = pl.program_id(0) * tm
    sc = _row_select(row0, tm, n_lat, sc_ref)
    sh = _row_select(row0, tm, n_lat, sh_ref)
    o_ref[...] = (y * (1.0 + sc) + sh).astype(o_ref.dtype)


def _normmod(x, g, sh, sc, n_lat, m=None):
    m = x.shape[0] if m is None else m
    d = x.shape[1]
    tm = _tile(m, 528, 16)
    vec = pl.BlockSpec((1, d), lambda i: (0, 0))
    two = pl.BlockSpec((2, d), lambda i: (0, 0))
    return pl.pallas_call(
        functools.partial(_normmod_kernel, tm=tm, n_lat=n_lat),
        grid=(m // tm,),
        in_specs=[pl.BlockSpec((tm, d), lambda i: (i, 0)), vec, two, two],
        out_specs=pl.BlockSpec((tm, d), lambda i: (i, 0)),
        out_shape=jax.ShapeDtypeStruct((m, d), BF16),
        compiler_params=_cparams(("arbitrary",)),
        name="normmod",
    )(x, g.reshape(1, d), sh, sc)


def _normmod_router_kernel(x_ref, g_ref, sh_ref, sc_ref, rw_ref, rb_ref, h_ref, w_ref, e_ref, *, n_exp):
    x = x_ref[...]
    y = x * lax.rsqrt(jnp.mean(x * x, axis=-1, keepdims=True) + EPS) * g_ref[...]
    h = y * (1.0 + sc_ref[0:1, :]) + sh_ref[0:1, :]
    h_ref[...] = h
    logits = jnp.dot(h, rw_ref[...], preferred_element_type=F32,
                     precision=lax.Precision.HIGHEST) + rb_ref[...]
    lane = lax.broadcasted_iota(jnp.int32, logits.shape, 1)
    lanef = lane.astype(F32)
    neg = jnp.float32(-jnp.inf)
    lg = jnp.where(lane < n_exp, logits, neg)
    m1 = jnp.max(lg, axis=-1, keepdims=True)
    i1 = jnp.min(jnp.where(lg == m1, lanef, float(LANES)), axis=-1, keepdims=True)
    lg2 = jnp.where(lanef == i1, neg, lg)
    m2 = jnp.max(lg2, axis=-1, keepdims=True)
    i2 = jnp.min(jnp.where(lg2 == m2, lanef, float(LANES)), axis=-1, keepdims=True)
    e = jnp.exp(m2 - m1)
    w1 = 1.0 / (1.0 + e)
    w2 = e / (1.0 + e)
    w_ref[...] = jnp.where(lane == 0, w1, jnp.where(lane == 1, w2, 0.0))
    e_ref[...] = jnp.where(lane == 0, i1, jnp.where(lane == 1, i2, 0.0)).astype(jnp.int32)


def _normmod_router(x, g, sh, sc, router_w, router_b, m):
    d = x.shape[1]
    n_exp = router_w.shape[1]
    tm = _tile(m, 256, 8)
    rw = jnp.zeros((d, LANES), F32).at[:, :n_exp].set(router_w)
    rb = jnp.zeros((1, LANES), F32).at[0, :n_exp].set(router_b)
    vec = pl.BlockSpec((1, d), lambda i: (0, 0))
    two = pl.BlockSpec((2, d), lambda i: (0, 0))
    return pl.pallas_call(
        functools.partial(_normmod_router_kernel, n_exp=n_exp),
        grid=(m // tm,),
        in_specs=[pl.BlockSpec((tm, d), lambda i: (i, 0)), vec, two, two,
                  pl.BlockSpec((d, LANES), lambda i: (0, 0)),
                  pl.BlockSpec((1, LANES), lambda i: (0, 0))],
        out_specs=[pl.BlockSpec((tm, d), lambda i: (i, 0)),
                   pl.BlockSpec((tm, LANES), lambda i: (i, 0)),
                   pl.BlockSpec((tm, LANES), lambda i: (i, 0))],
        out_shape=[jax.ShapeDtypeStruct((m, d), F32),
                   jax.ShapeDtypeStruct((m, LANES), F32),
                   jax.ShapeDtypeStruct((m, LANES), jnp.int32)],
        compiler_params=_cparams(("arbitrary",)),
        name="normmod_router",
    )(x, g.reshape(1, d), sh, sc, rw, rb)


def _mm_kernel(*refs, ks, act, has_res, tm, n_lat):
    n_a = len(ks)
    a_refs, w_ref = refs[:n_a], refs[n_a]
    o_ref = refs[-1]
    w = w_ref[...].astype(BF16)
    acc, off = None, 0
    for a_ref, k in zip(a_refs, ks):
        part = jnp.dot(a_ref[...], w[off:off + k], preferred_element_type=F32)
        acc = part if acc is None else acc + part
        off += k
    if act == "gelu":
        acc = jax.nn.gelu(acc, approximate=True)
    if has_res:
        x_ref, gate_ref = refs[n_a + 1], refs[n_a + 2]
        gate = _row_select(pl.program_id(0) * tm, tm, n_lat, gate_ref)
        acc = x_ref[...] + gate * acc
    o_ref[...] = acc.astype(o_ref.dtype)


def _mm(a_list, w, *, m, tm, tn, n_out, out_dtype, w_row_blk=0, w_col0=0, a_col_blk=None,
        act=None, resid=None, gate=None, n_lat=0, name="mm"):
    ks = tuple(k for _, k in a_list)
    arrs = [a for a, _ in a_list]
    kt = sum(ks)
    a_col_blk = a_col_blk or [0] * len(arrs)
    assert m % tm == 0 and n_out % tn == 0 and w_col0 % tn == 0
    c0 = w_col0 // tn
    in_specs = [pl.BlockSpec((tm, k), functools.partial(lambda i, j, cb: (i, cb), cb=cb))
                for k, cb in zip(ks, a_col_blk)]
    in_specs.append(pl.BlockSpec((kt, tn), lambda i, j: (w_row_blk, c0 + j)))
    args = arrs + [w]
    if resid is not None:
        in_specs += [pl.BlockSpec((tm, tn), lambda i, j: (i, j)),
                     pl.BlockSpec((2, tn), lambda i, j: (0, j))]
        args += [resid, gate]
    return pl.pallas_call(
        functools.partial(_mm_kernel, ks=ks, act=act, has_res=resid is not None, tm=tm, n_lat=n_lat),
        grid=(m // tm, n_out // tn),
        in_specs=in_specs,
        out_specs=pl.BlockSpec((tm, tn), lambda i, j: (i, j)),
        out_shape=jax.ShapeDtypeStruct((m, n_out), out_dtype),
        compiler_params=_cparams(("arbitrary", "arbitrary")),
        name=name,
    )(*args)


def _qkv_kernel(a_ref, w_ref, g_ref, cos_ref, sa_ref, sb_ref, o_ref, *, v_tiles, hd):
    y = jnp.dot(a_ref[...], w_ref[...].astype(BF16), preferred_element_type=F32)
    j = pl.program_id(1)
    is_v = functools.reduce(jnp.logical_or, [j == t for t in v_tiles])

    @pl.when(is_v)
    def _():
        o_ref[...] = y.astype(o_ref.dtype)

    @pl.when(jnp.logical_not(is_v))
    def _():
        cos, sa, sb = cos_ref[...], sa_ref[...], sb_ref[...]
        for h in range(y.shape[1] // hd):
            yh = y[:, h * hd:(h + 1) * hd]
            yh = yh * lax.rsqrt(jnp.mean(yh * yh, axis=-1, keepdims=True) + EPS) * g_ref[:, h * hd:(h + 1) * hd]
            out = yh * cos + pltpu.roll(yh, hd - 1, 1) * sa + pltpu.roll(yh, 1, 1) * sb
            o_ref[:, h * hd:(h + 1) * hd] = out.astype(o_ref.dtype)


def _qkv(h, w_in, gains, cos, sa, sb, v_tiles, tm, tn):
    m, d = h.shape
    n = w_in.shape[1]
    tab = pl.BlockSpec((tm, HEAD_DIM), lambda i, j: (i, 0))
    return pl.pallas_call(
        functools.partial(_qkv_kernel, v_tiles=v_tiles, hd=HEAD_DIM),
        grid=(m // tm, n // tn),
        in_specs=[pl.BlockSpec((tm, d), lambda i, j: (i, 0)),
                  pl.BlockSpec((d, tn), lambda i, j: (0, j)),
                  pl.BlockSpec((1, tn), lambda i, j: (0, j)),
                  tab, tab, tab],
        out_specs=pl.BlockSpec((tm, tn), lambda i, j: (i, j)),
        out_shape=jax.ShapeDtypeStruct((m, n), BF16),
        compiler_params=_cparams(("arbitrary", "arbitrary")),
        name="qkv_proj",
    )(h, w_in, gains, cos, sa, sb)


def _gateup_kernel(a_ref, wg_ref, wu_ref, o_ref):
    a = a_ref[...]
    g = jnp.dot(a, wg_ref[...].astype(BF16), preferred_element_type=F32)
    u = jnp.dot(a, wu_ref[...].astype(BF16), preferred_element_type=F32)
    o_ref[...] = (g * jax.nn.sigmoid(g) * u).astype(o_ref.dtype)


def _gateup(h, wg, wu, tm, tn):
    m, d = h.shape
    n = wg.shape[1]
    wspec = pl.BlockSpec((d, tn), lambda i, j: (0, j))
    return pl.pallas_call(
        _gateup_kernel,
        grid=(m // tm, n // tn),
        in_specs=[pl.BlockSpec((tm, d), lambda i, j: (i, 0)), wspec, wspec],
        out_specs=pl.BlockSpec((tm, tn), lambda i, j: (i, j)),
        out_shape=jax.ShapeDtypeStruct((m, n), BF16),
        compiler_params=_cparams(("arbitrary", "arbitrary")),
        name="ffn_gateup",
    )(h, wg, wu)


def _flash_head(q, k_ref, v_ref, tk, n_chunks, dv):
    tq = q.shape[0]

    def body(c, carry):
        m, l, acc = carry
        start = pl.multiple_of(c * tk, tk)
        k = k_ref[pl.ds(start, tk), :]
        v = v_ref[pl.ds(start, tk), :]
        s = lax.dot_general(q, k, NT_DIMS, preferred_element_type=F32)
        m_new = jnp.maximum(m, jnp.max(s, axis=-1, keepdims=True))
        alpha = jnp.exp(m - m_new)
        p = jnp.exp(s - m_new)
        l = alpha * l + jnp.sum(p, axis=-1, keepdims=True)
        acc = alpha * acc + jnp.dot(p.astype(BF16), v, preferred_element_type=F32)
        return m_new, l, acc

    init = (jnp.full((tq, 1), -jnp.inf, F32), jnp.zeros((tq, 1), F32), jnp.zeros((tq, dv), F32))
    _, l, acc = lax.fori_loop(0, n_chunks, body, init)
    return acc, l


def _gqa_kernel(q_ref, k_ref, v_ref, o_ref, *, tk, n_chunks, n_rep, hd):
    for h in range(n_rep):
        acc, l = _flash_head(q_ref[:, h * hd:(h + 1) * hd], k_ref, v_ref, tk, n_chunks, hd)
        o_ref[:, h * hd:(h + 1) * hd] = (acc / l).astype(o_ref.dtype)


def _gqa_attention(qkv, *, n_q, q_blk0, n_k, k_blk0, tq, tk, n_kv, n_rep, q_col0, k_col0, v_col0):
    hd = HEAD_DIM
    qw = n_rep * hd
    return pl.pallas_call(
        functools.partial(_gqa_kernel, tk=tk, n_chunks=n_k // tk, n_rep=n_rep, hd=hd),
        grid=(n_kv, n_q // tq),
        in_specs=[pl.BlockSpec((tq, qw), lambda g, i: (q_blk0 + i, q_col0 // qw + g)),
                  pl.BlockSpec((n_k, hd), lambda g, i: (k_blk0, k_col0 // hd + g)),
                  pl.BlockSpec((n_k, hd), lambda g, i: (k_blk0, v_col0 // hd + g))],
        out_specs=pl.BlockSpec((tq, qw), lambda g, i: (i, g)),
        out_shape=jax.ShapeDtypeStruct((n_q, n_kv * qw), BF16),
        compiler_params=_cparams(("arbitrary", "arbitrary")),
        name="gqa_attention",
    )(qkv, qkv, qkv)


def _diff_kernel(q_ref, k_ref, v_ref, lam_ref, sub_ref, o_ref, *, tk, n_chunks, hd, lam_init):
    dv = v_ref.shape[1]
    outs = []
    for mi in range(2):
        acc, l = _flash_head(q_ref[:, mi * hd:(mi + 1) * hd], k_ref.at[:, mi * hd:(mi + 1) * hd], v_ref,
                             tk, n_chunks, dv)
        outs.append(acc / l)
    lp = lam_ref[...]
    lam = (jnp.exp(jnp.sum(lp[0:1] * lp[1:2], axis=-1, keepdims=True))
           - jnp.exp(jnp.sum(lp[2:3] * lp[3:4], axis=-1, keepdims=True)) + lam_init)
    o = outs[0] - lam * outs[1]
    o = o * lax.rsqrt(jnp.mean(o * o, axis=-1, keepdims=True) + EPS) * sub_ref[...] * (1.0 - lam_init)
    o_ref[...] = o.astype(o_ref.dtype)


def _diff_attention(qkv, lam_params, subln, *, n_q, q_blk0, n_k, k_blk0, tq, tk, n_heads,
                    q_col0, k_col0, v_col0, lam_init):
    hd = HEAD_DIM
    w2 = 2 * hd
    return pl.pallas_call(
        functools.partial(_diff_kernel, tk=tk, n_chunks=n_k // tk, hd=hd, lam_init=lam_init),
        grid=(n_heads, n_q // tq),
        in_specs=[pl.BlockSpec((tq, w2), lambda h, i: (q_blk0 + i, q_col0 // w2 + h)),
                  pl.BlockSpec((n_k, w2), lambda h, i: (k_blk0, k_col0 // w2 + h)),
                  pl.BlockSpec((n_k, w2), lambda h, i: (k_blk0, v_col0 // w2 + h)),
                  pl.BlockSpec((8, hd), lambda h, i: (0, 0)),
                  pl.BlockSpec((1, w2), lambda h, i: (0, 0))],
        out_specs=pl.BlockSpec((tq, w2), lambda h, i: (i, h)),
        out_shape=jax.ShapeDtypeStruct((n_q, n_heads * w2), BF16),
        compiler_params=_cparams(("arbitrary", "arbitrary")),
        name="diff_attention",
    )(qkv, qkv, qkv, lam_params, subln)


def _scan_chunk(a, b, h0, reverse):
    n = a.shape[0]
    row = lax.broadcasted_iota(jnp.int32, (n, 1), 0)
    d = 1
    while d < n:
        if reverse:
            a_sh, b_sh, keep = pltpu.roll(a, n - d, 0), pltpu.roll(b, n - d, 0), row < n - d
        else:
            a_sh, b_sh, keep = pltpu.roll(a, d, 0), pltpu.roll(b, d, 0), row >= d
        b = jnp.where(keep, a * b_sh + b, b)
        a = jnp.where(keep, a * a_sh, a)
        d *= 2
    h = a * h0 + b
    return h, (h[0:1] if reverse else h[n - 1:n])


def _rglru_kernel(x_ref, gy_ref, cw_ref, cb_ref, gaw_ref, gab_ref, gxw_ref, gxb_ref, lam_ref, o_ref, s_ref,
                  *, n_lat, n_ctx, tc):
    n_rows = n_lat + n_ctx
    cw, cb = cw_ref[...], cb_ref[...]
    z = -lam_ref[...]
    softplus = jnp.maximum(z, 0.0) + jnp.log(1.0 + jnp.exp(-jnp.abs(z)))

    def conv(t0, n, seg0, seg1):
        xs = x_ref[pl.ds(t0, n), :]
        prev = x_ref[pl.ds(pl.multiple_of(jnp.maximum(t0 - 8, 0), 8), 8), :]
        nxt = x_ref[pl.ds(pl.multiple_of(jnp.minimum(t0 + n, n_rows - 8), 8), 8), :]
        prev = jnp.where(t0 > seg0, prev, 0.0)
        nxt = jnp.where(t0 + n < seg1, nxt, 0.0)
        ext = jnp.concatenate([prev, xs, nxt], axis=0)
        return (ext[6:6 + n] * cw[0:1] + ext[7:7 + n] * cw[1:2] + ext[8:8 + n] * cw[2:3]
                + ext[9:9 + n] * cw[3:4] + cb)

    for d in range(2):
        reverse = d == 1
        wa = gaw_ref[d, 0].astype(BF16)
        wx = gxw_ref[d, 0].astype(BF16)
        ba, bx = gab_ref[d:d + 1, :], gxb_ref[d:d + 1, :]
        c8 = -RG_C * softplus[d:d + 1, :]

        def chunk(t0, n, seg0, seg1, h0, reverse=reverse, wa=wa, wx=wx, ba=ba, bx=bx, c8=c8):
            xc = conv(t0, n, seg0, seg1)
            xb = xc.astype(BF16)
            r = jax.nn.sigmoid(jnp.dot(xb, wa, preferred_element_type=F32) + ba)
            i = jax.nn.sigmoid(jnp.dot(xb, wx, preferred_element_type=F32) + bx)
            a = jnp.exp(c8 * r)
            b = jnp.sqrt(1.0 - a * a) * (i * xc)
            return _scan_chunk(a, b, h0, reverse)

        _, h_end = chunk(n_lat, n_ctx, n_lat, n_rows, jnp.zeros((1, cw.shape[1]), F32))
        n_ch = n_lat // tc

        def body(ci, hc, chunk=chunk, reverse=reverse):
            cidx = (n_ch - 1 - ci) if reverse else ci
            t0 = pl.multiple_of(cidx * tc, tc)
            h, hc = chunk(t0, tc, 0, n_lat, hc)
            if reverse:
                s = s_ref[pl.ds(t0, tc), :] + h
                o_ref[pl.ds(t0, tc), :] = (s * gy_ref[pl.ds(t0, tc), :].astype(F32)).astype(o_ref.dtype)
            else:
                s_ref[pl.ds(t0, tc), :] = h
            return hc

        lax.fori_loop(0, n_ch, body, h_end)


def _rglru(xr, gy, conv_w, conv_b, ga_w, ga_b, gx_w, gx_b, lam, n_lat, n_ctx):
    n_rows, c_tot = xr.shape
    nb, bw = ga_w.shape[1], ga_w.shape[2]
    tc = _tile(n_lat, 512, 8)
    col = lambda j: (0, j)
    return pl.pallas_call(
        functools.partial(_rglru_kernel, n_lat=n_lat, n_ctx=n_ctx, tc=tc),
        grid=(nb,),
        in_specs=[pl.BlockSpec((n_rows, bw), col),
                  pl.BlockSpec((n_lat, bw), col),
                  pl.BlockSpec((conv_w.shape[0], bw), col),
                  pl.BlockSpec((1, bw), col),
                  pl.BlockSpec((2, 1, bw, bw), lambda j: (0, j, 0, 0)),
                  pl.BlockSpec((2, bw), col),
                  pl.BlockSpec((2, 1, bw, bw), lambda j: (0, j, 0, 0)),
                  pl.BlockSpec((2, bw), col),
                  pl.BlockSpec((2, bw), col)],
        out_specs=pl.BlockSpec((n_lat, bw), col),
        out_shape=jax.ShapeDtypeStruct((n_lat, c_tot), BF16),
        scratch_shapes=[pltpu.VMEM((n_lat, bw), F32)],
        compiler_params=_cparams(("arbitrary",)),
        name="rglru",
    )(xr, gy, conv_w, conv_b.reshape(1, c_tot), ga_w, ga_b, gx_w, gx_b, lam)


def _row_copy(src_hbm, row, dst, r, sem):
    return pltpu.make_async_copy(src_hbm.at[pl.ds(row, 1)], dst.at[pl.ds(r, 1)], sem)


def _gather_kernel(src_ref, h_hbm, o_ref, buf, sem, *, tm):
    base = pl.program_id(0) * tm

    def start(r, c):
        _row_copy(h_hbm, src_ref[base + r], buf, r, sem).start()
        return c

    def wait(r, c):
        _row_copy(h_hbm, 0, buf, r, sem).wait()
        return c

    lax.fori_loop(0, tm, start, 0)
    lax.fori_loop(0, tm, wait, 0)
    o_ref[...] = buf[...].astype(o_ref.dtype)


def _moe_gather(h, src_tok, n_rows, tm):
    d = h.shape[1]
    return pl.pallas_call(
        functools.partial(_gather_kernel, tm=tm),
        grid_spec=pltpu.PrefetchScalarGridSpec(
            num_scalar_prefetch=1,
            grid=(n_rows // tm,),
            in_specs=[pl.BlockSpec(memory_space=pl.ANY)],
            out_specs=pl.BlockSpec((tm, d), lambda t, src: (t, 0)),
            scratch_shapes=[pltpu.VMEM((tm, d), F32), pltpu.SemaphoreType.DMA(())]),
        out_shape=jax.ShapeDtypeStruct((n_rows, d), BF16),
        compiler_params=_cparams(("arbitrary",)),
        name="moe_gather",
    )(src_tok, h)


def _expert_changed(te_ref, t):
    return jnp.logical_or(t == 0, te_ref[t] != te_ref[jnp.maximum(t - 1, 0)])


def _gmm1_kernel(te_ref, x_ref, wg_ref, wu_ref, o_ref, wgb, wub):
    @pl.when(_expert_changed(te_ref, pl.program_id(1)))
    def _():
        wgb[...] = wg_ref[0].astype(BF16)
        wub[...] = wu_ref[0].astype(BF16)

    x = x_ref[...]
    g = jnp.dot(x, wgb[...], preferred_element_type=F32)
    u = jnp.dot(x, wub[...], preferred_element_type=F32)
    o_ref[...] = (g * jax.nn.sigmoid(g) * u).astype(o_ref.dtype)


def _gmm2_kernel(te_ref, h_ref, wd_ref, rw_ref, o_ref, wdb):
    @pl.when(_expert_changed(te_ref, pl.program_id(1)))
    def _():
        wdb[...] = wd_ref[0].astype(BF16)

    o_ref[...] = jnp.dot(h_ref[...], wdb[...], preferred_element_type=F32) * rw_ref[...]


def _moe_gmm1(tile_e, xs, wg, wu, tm, tn):
    n_rows, d = xs.shape
    ff = wg.shape[2]
    wspec = pl.BlockSpec((1, d, tn), lambda j, t, te: (te[t], 0, j))
    return pl.pallas_call(
        _gmm1_kernel,
        grid_spec=pltpu.PrefetchScalarGridSpec(
            num_scalar_prefetch=1,
            grid=(ff // tn, n_rows // tm),
            in_specs=[pl.BlockSpec((tm, d), lambda j, t, te: (t, 0)), wspec, wspec],
            out_specs=pl.BlockSpec((tm, tn), lambda j, t, te: (t, j)),
            scratch_shapes=[pltpu.VMEM((d, tn), BF16), pltpu.VMEM((d, tn), BF16)]),
        out_shape=jax.ShapeDtypeStruct((n_rows, ff), BF16),
        compiler_params=_cparams(("arbitrary", "arbitrary")),
        name="moe_gateup",
    )(tile_e, xs, wg, wu)


def _moe_gmm2(tile_e, hs, wd, row_w, tm, tn):
    n_rows, ff = hs.shape
    d = wd.shape[2]
    return pl.pallas_call(
        _gmm2_kernel,
        grid_spec=pltpu.PrefetchScalarGridSpec(
            num_scalar_prefetch=1,
            grid=(d // tn, n_rows // tm),
            in_specs=[pl.BlockSpec((tm, ff), lambda j, t, te: (t, 0)),
                      pl.BlockSpec((1, ff, tn), lambda j, t, te: (te[t], 0, j)),
                      pl.BlockSpec((tm, 1), lambda j, t, te: (t, 0))],
            out_specs=pl.BlockSpec((tm, tn), lambda j, t, te: (t, j)),
            scratch_shapes=[pltpu.VMEM((ff, tn), BF16)]),
        out_shape=jax.ShapeDtypeStruct((n_rows, d), F32),
        compiler_params=_cparams(("arbitrary", "arbitrary")),
        name="moe_down",
    )(tile_e, hs, wd, row_w)


def _combine_kernel(dest_ref, x_ref, g_ref, y_hbm, o_ref, b1, b2, sem, *, tt):
    base = pl.program_id(0) * tt

    def start(r, c):
        _row_copy(y_hbm, dest_ref[2 * (base + r)], b1, r, sem).start()
        _row_copy(y_hbm, dest_ref[2 * (base + r) + 1], b2, r, sem).start()
        return c

    def wait(r, c):
        _row_copy(y_hbm, 0, b1, r, sem).wait()
        _row_copy(y_hbm, 0, b2, r, sem).wait()
        return c

    lax.fori_loop(0, tt, start, 0)
    lax.fori_loop(0, tt, wait, 0)
    o_ref[...] = x_ref[...] + g_ref[0:1, :] * (b1[...] + b2[...])


def _moe_combine(dest, x, gate, y, n_tok, tt):
    d = x.shape[1]
    return pl.pallas_call(
        functools.partial(_combine_kernel, tt=tt),
        grid_spec=pltpu.PrefetchScalarGridSpec(
            num_scalar_prefetch=1,
            grid=(n_tok // tt,),
            in_specs=[pl.BlockSpec((tt, d), lambda i, dst: (i, 0)),
                      pl.BlockSpec((2, d), lambda i, dst: (0, 0)),
                      pl.BlockSpec(memory_space=pl.ANY)],
            out_specs=pl.BlockSpec((tt, d), lambda i, dst: (i, 0)),
            scratch_shapes=[pltpu.VMEM((tt, d), F32), pltpu.VMEM((tt, d), F32),
                            pltpu.SemaphoreType.DMA(())]),
        out_shape=jax.ShapeDtypeStruct((n_tok, d), F32),
        compiler_params=_cparams(("arbitrary",)),
        name="moe_combine",
    )(dest, x, gate, y)


def _moe_plan(e_idx, w_top, n_exp, tm):
    n_tok = e_idx.shape[0]
    n_slot = n_tok * TOP_K
    flat_e = e_idx.reshape(n_slot)
    onehot = (flat_e[:, None] == jnp.arange(n_exp, dtype=jnp.int32)[None, :]).astype(jnp.int32)
    csum = jnp.cumsum(onehot, axis=0)
    rank = jnp.sum(csum * onehot, axis=1) - 1
    counts = csum[-1]
    padded = ((counts + tm - 1) // tm) * tm
    ends = jnp.cumsum(padded)
    starts = ends - padded
    dest = (starts[flat_e] + rank).astype(jnp.int32)
    n_tiles = n_slot // tm + n_exp
    n_rows = n_tiles * tm
    tile_start = jnp.arange(n_tiles, dtype=jnp.int32) * tm
    tile_e = jnp.minimum(jnp.searchsorted(ends, tile_start, side="right"), n_exp - 1).astype(jnp.int32)
    src_tok = jnp.zeros((n_rows,), jnp.int32).at[dest].set(jnp.arange(n_slot, dtype=jnp.int32) // TOP_K)
    row_w = jnp.zeros((n_rows, 1), F32).at[dest, 0].set(w_top.reshape(n_slot))
    return dest, src_tok, row_w, tile_e, n_rows


def _rope_tables(n_lat, n_ctx):
    half = HEAD_DIM // 2
    inv = ROPE_THETA ** (-jnp.arange(0, half, 2, dtype=F32) / half)
    rows = n_lat // GRID_W
    row = jnp.repeat(jnp.arange(rows, dtype=F32), GRID_W)
    col = jnp.tile(jnp.arange(GRID_W, dtype=F32), rows)
    ang = jnp.concatenate([row[:, None] * inv, col[:, None] * inv], axis=-1)
    cos = jnp.repeat(jnp.cos(ang), 2, axis=-1)
    sin = jnp.repeat(jnp.sin(ang), 2, axis=-1)
    even = (jnp.arange(HEAD_DIM) % 2 == 0)[None, :]
    sa = jnp.where(even, -sin, 0.0)
    sb = jnp.where(even, 0.0, sin)
    ident = jnp.ones((n_ctx, HEAD_DIM), F32)
    zero = jnp.zeros((n_ctx, HEAD_DIM), F32)
    return (jnp.concatenate([cos, ident]), jnp.concatenate([sa, zero]), jnp.concatenate([sb, zero]))


def kernel(x, c, ctx, c_ctx, attn_w_mod, attn_b_mod, attn_norm_mix, attn_norm_ffn, attn_w_in, attn_gqa_q_norm, attn_gqa_k_norm, attn_diff_q_norm, attn_diff_k_norm, attn_diff_lambda_q1, attn_diff_lambda_k1, attn_diff_lambda_q2, attn_diff_lambda_k2, attn_diff_subln, attn_w_out, ffn_w_gate, ffn_w_up, ffn_w_down, rec_w_mod, rec_b_mod, rec_norm_mix, rec_norm_ffn, rec_w_in, rec_conv_w, rec_conv_b, rec_gate_a_w, rec_gate_a_b, rec_gate_x_w, rec_gate_x_b, rec_lru_lambda, rec_w_out, moe_router_w, moe_router_b, moe_w_gate, moe_w_up, moe_w_down):
    assert x.shape[0] == 1 and attn_w_in.shape[0] == 1 and rec_w_in.shape[0] == 1
    n_lat, d = x.shape[1], x.shape[2]
    n_ctx = ctx.shape[1]
    n_rows = n_lat + n_ctx
    hd = HEAD_DIM
    gqa_heads, diff_heads = d // 256, d // 512
    kv_heads = gqa_heads // 4
    n_rep = gqa_heads // kv_heads
    gq_w, gkv_w = gqa_heads * hd, kv_heads * hd
    dq_w, dv_w = diff_heads * 2 * hd, diff_heads * 2 * hd
    col_ka, col_va, col_qb = gq_w, gq_w + gkv_w, gq_w + 2 * gkv_w
    col_kb, col_vb = col_qb + dq_w, col_qb + 2 * dq_w
    d_ff = ffn_w_gate.shape[2]
    n_exp = moe_router_w.shape[2]

    xs = jnp.concatenate([x[0], ctx[0]], axis=0)
    c2 = jnp.zeros((8, d), F32).at[0].set(c[0]).at[1].set(c_ctx)
    tm_big = _tile(n_rows, 1056, 16)

    mods = _modvec(c2, attn_w_mod[0], attn_b_mod[0])[:2].reshape(2, 6, d)
    sh1, sc1, g1, sh2, sc2, g2 = (mods[:, j] for j in range(6))
    h = _normmod(xs, attn_norm_mix[0], sh1, sc1, n_lat)

    scale = hd ** -0.5
    tn_qkv = min(4 * hd, gkv_w)
    gains = jnp.concatenate([
        jnp.tile(attn_gqa_q_norm[0] * scale, gqa_heads), jnp.tile(attn_gqa_k_norm[0], kv_heads),
        jnp.ones((gkv_w,), F32),
        jnp.tile((attn_diff_q_norm[0] * scale).reshape(-1), diff_heads),
        jnp.tile(attn_diff_k_norm[0].reshape(-1), diff_heads), jnp.ones((dv_w,), F32)]).reshape(1, -1)
    v_tiles = tuple(range(col_va // tn_qkv, col_qb // tn_qkv)) + tuple(
        range(col_vb // tn_qkv, (col_vb + dv_w) // tn_qkv))
    cos, sa, sb = _rope_tables(n_lat, n_ctx)
    qkv = _qkv(h, attn_w_in[0], gains, cos, sa, sb, v_tiles, tm_big, tn_qkv)

    tq = _tile(n_ctx, 256, 16)
    tk_lat = _tile(n_rows, 1536, LANES)
    layer = 0
    lam_init = 0.8 - 0.6 * math.exp(-0.3 * layer)
    lam_params = jnp.zeros((8, hd), F32).at[0].set(attn_diff_lambda_q1[0]).at[1].set(attn_diff_lambda_k1[0]) \
        .at[2].set(attn_diff_lambda_q2[0]).at[3].set(attn_diff_lambda_k2[0])
    subln = attn_diff_subln[0].reshape(1, 2 * hd)
    gqa_args = dict(tq=tq, n_kv=kv_heads, n_rep=n_rep, q_col0=0, k_col0=col_ka, v_col0=col_va)
    diff_args = dict(tq=tq, n_heads=diff_heads, q_col0=col_qb, k_col0=col_kb, v_col0=col_vb, lam_init=lam_init)
    ctx_blk = n_lat // n_ctx
    oa = jnp.concatenate([
        _gqa_attention(qkv, n_q=n_lat, q_blk0=0, n_k=n_rows, k_blk0=0, tk=tk_lat, **gqa_args),
        _gqa_attention(qkv, n_q=n_ctx, q_blk0=n_lat // tq, n_k=n_ctx, k_blk0=ctx_blk, tk=n_ctx, **gqa_args)])
    ob = jnp.concatenate([
        _diff_attention(qkv, lam_params, subln, n_q=n_lat, q_blk0=0, n_k=n_rows, k_blk0=0, tk=tk_lat, **diff_args),
        _diff_attention(qkv, lam_params, subln, n_q=n_ctx, q_blk0=n_lat // tq, n_k=n_ctx, k_blk0=ctx_blk,
                        tk=n_ctx, **diff_args)])

    x1 = _mm([(oa, gq_w), (ob, dv_w)], attn_w_out[0], m=n_rows, tm=tm_big, tn=512, n_out=d, out_dtype=F32,
             resid=xs, gate=g1, n_lat=n_lat, name="attn_out")
    h = _normmod(x1, attn_norm_ffn[0], sh2, sc2, n_lat)
    hff = _gateup(h, ffn_w_gate[0], ffn_w_up[0], tm_big, 256)
    k_half = d_ff // 2
    x2 = x1
    for kb in range(2):
        x2 = _mm([(hff, k_half)], ffn_w_down[0], m=n_rows, tm=tm_big, tn=256, n_out=d, out_dtype=F32,
                 w_row_blk=kb, a_col_blk=[kb], resid=x2, gate=g2, n_lat=n_lat, name="ffn_down")

    mods = _modvec(c2, rec_w_mod[0], rec_b_mod[0])[:2].reshape(2, 6, d)
    sh1, sc1, g1, sh2, sc2, g2 = (mods[:, j] for j in range(6))
    h = _normmod(x2, rec_norm_mix[0], sh1, sc1, n_lat)
    d_rnn = rec_w_out.shape[1]
    gy = _mm([(h, d)], rec_w_in[0], m=n_rows, tm=tm_big, tn=512, n_out=d_rnn, out_dtype=BF16,
             act="gelu", name="rec_in_y")
    xr = _mm([(h, d)], rec_w_in[0], m=n_rows, tm=tm_big, tn=512, n_out=d_rnn, out_dtype=F32,
             w_col0=d_rnn, name="rec_in_x")
    s = _rglru(xr, gy, rec_conv_w[0], rec_conv_b[0], rec_gate_a_w[0], rec_gate_a_b[0],
               rec_gate_x_w[0], rec_gate_x_b[0], rec_lru_lambda[0], n_lat, n_ctx)
    tm_lat = _tile(n_lat, 1024, 16)
    x3 = _mm([(s, d_rnn)], rec_w_out[0], m=n_lat, tm=tm_lat, tn=512, n_out=d, out_dtype=F32,
             resid=x2, gate=g1, n_lat=n_lat, name="rec_out")

    h4, w_top, e_top = _normmod_router(x3, rec_norm_ffn[0], sh2, sc2, moe_router_w[0], moe_router_b[0], n_lat)
    tm_e = 256
    dest, src_tok, row_w, tile_e, n_sorted = _moe_plan(e_top[:, :TOP_K], w_top[:, :TOP_K], n_exp, tm_e)
    xg = _moe_gather(h4, src_tok, n_sorted, tm_e)
    hs = _moe_gmm1(tile_e, xg, moe_w_gate[0], moe_w_up[0], tm_e, 512)
    ys = _moe_gmm2(tile_e, hs, moe_w_down[0], row_w, tm_e, 512)
    out = _moe_combine(dest, x3, g2, ys, n_lat, 256)
    return out.reshape(1, n_lat, d)
```

```python
import functools
import math

import jax
import jax.numpy as jnp
from jax import lax
from jax.experimental import pallas as pl
from jax.experimental.pallas import tpu as pltpu

F32 = jnp.float32
BF16 = jnp.bfloat16
EPS = 1e-6
ROPE_THETA = 10000.0
GRID_W = 64
HEAD_DIM = 128
RG_C = 8.0
TOP_K = 2
LANES = 128
VMEM_LIMIT = 56 * 1024 * 1024
NT_DIMS = (((1,), (1,)), ((), ()))
A_RESIDENT = pl.Buffered(1)


def _cparams(sem, vmem=VMEM_LIMIT):
    return pltpu.CompilerParams(dimension_semantics=sem, vmem_limit_bytes=vmem)


def _tile(n, target, mult):
    best = None
    for t in range(mult, min(n, target) + 1, mult):
        if n % t == 0:
            best = t
    assert best is not None, (n, target, mult)
    return best


def _row_select(row0, tm, n_lat, ref):
    row = row0 + lax.broadcasted_iota(jnp.int32, (tm, 1), 0)
    return jnp.where(row < n_lat, ref[0:1, :], ref[1:2, :])


def _modvec_kernel(c_ref, w_ref, b_ref, o_ref):
    c = c_ref[...]
    a = (c * jax.nn.sigmoid(c)).astype(BF16)
    o_ref[...] = jnp.dot(a, w_ref[...].astype(BF16), preferred_element_type=F32) + b_ref[...]


def _modvec(c2, w_mod, b_mod):
    d, n = w_mod.shape
    tn = _tile(n, 512, LANES)
    return pl.pallas_call(
        _modvec_kernel,
        grid=(n // tn,),
        in_specs=[pl.BlockSpec((8, d), lambda j: (0, 0)),
                  pl.BlockSpec((d, tn), lambda j: (0, j)),
                  pl.BlockSpec((1, tn), lambda j: (0, j))],
        out_specs=pl.BlockSpec((8, tn), lambda j: (0, j)),
        out_shape=jax.ShapeDtypeStruct((8, n), F32),
        compiler_params=_cparams(("arbitrary",)),
        name="modvec",
    )(c2, w_mod, b_mod.reshape(1, n))


def _normmod_kernel(x_ref, g_ref, sh_ref, sc_ref, o_ref, *, tm, n_lat):
    x = x_ref[...]
    y = x * lax.rsqrt(jnp.mean(x * x, axis=-1, keepdims=True) + EPS) * g_ref[...]
    row0 = pl.program_id(0) * tm
    sc = _row_select(row0, tm, n_lat, sc_ref)
    sh = _row_select(row0, tm, n_lat, sh_ref)
    o_ref[...] = (y * (1.0 + sc) + sh).astype(o_ref.dtype)


def _normmod(x, g, sh, sc, n_lat, m=None):
    m = x.shape[0] if m is None else m
    d = x.shape[1]
    tm = _tile(m, 528, 16)
    vec = pl.BlockSpec((1, d), lambda i: (0, 0))
    two = pl.BlockSpec((2, d), lambda i: (0, 0))
    return pl.pallas_call(
        functools.partial(_normmod_kernel, tm=tm, n_lat=n_lat),
        grid=(m // tm,),
        in_specs=[pl.BlockSpec((tm, d), lambda i: (i, 0)), vec, two, two],
        out_specs=pl.BlockSpec((tm, d), lambda i: (i, 0)),
        out_shape=jax.ShapeDtypeStruct((m, d), BF16),
        compiler_params=_cparams(("arbitrary",)),
        name="normmod",
    )(x, g.reshape(1, d), sh, sc)


def _normmod_router_kernel(x_ref, g_ref, sh_ref, sc_ref, rw_ref, rb_ref, h_ref, w_ref, e_ref, *, n_exp):
    x = x_ref[...]
    y = x * lax.rsqrt(jnp.mean(x * x, axis=-1, keepdims=True) + EPS) * g_ref[...]
    h = y * (1.0 + sc_ref[0:1, :]) + sh_ref[0:1, :]
    h_ref[...] = h
    logits = jnp.dot(h, rw_ref[...], preferred_element_type=F32,
                     precision=lax.Precision.HIGHEST) + rb_ref[...]
    lane = lax.broadcasted_iota(jnp.int32, logits.shape, 1)
    lanef = lane.astype(F32)
    neg = jnp.float32(-jnp.inf)
    lg = jnp.where(lane < n_exp, logits, neg)
    m1 = jnp.max(lg, axis=-1, keepdims=True)
    i1 = jnp.min(jnp.where(lg == m1, lanef, float(LANES)), axis=-1, keepdims=True)
    lg2 = jnp.where(lanef == i1, neg, lg)
    m2 = jnp.max(lg2, axis=-1, keepdims=True)
    i2 = jnp.min(jnp.where(lg2 == m2, lanef, float(LANES)), axis=-1, keepdims=True)
    e = jnp.exp(m2 - m1)
    w1 = 1.0 / (1.0 + e)
    w2 = e / (1.0 + e)
    w_ref[...] = jnp.where(lane == 0, w1, jnp.where(lane == 1, w2, 0.0))
    e_ref[...] = jnp.where(lane == 0, i1, jnp.where(lane == 1, i2, 0.0)).astype(jnp.int32)


def _normmod_router(x, g, sh, sc, router_w, router_b, m):
    d = x.shape[1]
    n_exp = router_w.shape[1]
    tm = _tile(m, 256, 8)
    rw = jnp.zeros((d, LANES), F32).at[:, :n_exp].set(router_w)
    rb = jnp.zeros((1, LANES), F32).at[0, :n_exp].set(router_b)
    vec = pl.BlockSpec((1, d), lambda i: (0, 0))
    two = pl.BlockSpec((2, d), lambda i: (0, 0))
    return pl.pallas_call(
        functools.partial(_normmod_router_kernel, n_exp=n_exp),
        grid=(m // tm,),
        in_specs=[pl.BlockSpec((tm, d), lambda i: (i, 0)), vec, two, two,
                  pl.BlockSpec((d, LANES), lambda i: (0, 0)),
                  pl.BlockSpec((1, LANES), lambda i: (0, 0))],
        out_specs=[pl.BlockSpec((tm, d), lambda i: (i, 0)),
                   pl.BlockSpec((tm, LANES), lambda i: (i, 0)),
                   pl.BlockSpec((tm, LANES), lambda i: (i, 0))],
        out_shape=[jax.ShapeDtypeStruct((m, d), F32),
                   jax.ShapeDtypeStruct((m, LANES), F32),
                   jax.ShapeDtypeStruct((m, LANES), jnp.int32)],
        compiler_params=_cparams(("arbitrary",)),
        name="normmod_router",
    )(x, g.reshape(1, d), sh, sc, rw, rb)


def _mm_kernel(*refs, ks, act, has_res, tm, n_lat):
    n_a = len(ks)
    a_refs, w_ref = refs[:n_a], refs[n_a]
    o_ref = refs[-1]
    w = w_ref[...].astype(BF16)
    acc, off = None, 0
    for a_ref, k in zip(a_refs, ks):
        part = jnp.dot(a_ref[...], w[off:off + k], preferred_element_type=F32)
        acc = part if acc is None else acc + part
        off += k
    if act == "gelu":
        acc = jax.nn.gelu(acc, approximate=True)
    if has_res:
        x_ref, gate_ref = refs[n_a + 1], refs[n_a + 2]
        gate = _row_select(pl.program_id(0) * tm, tm, n_lat, gate_ref)
        acc = x_ref[...] + gate * acc
    o_ref[...] = acc.astype(o_ref.dtype)


def _mm(a_list, w, *, m, tm, tn, n_out, out_dtype, w_row_blk=0, w_col0=0, a_col_blk=None,
        act=None, resid=None, gate=None, n_lat=0, name="mm"):
    ks = tuple(k for _, k in a_list)
    arrs = [a for a, _ in a_list]
    kt = sum(ks)
    a_col_blk = a_col_blk or [0] * len(arrs)
    assert m % tm == 0 and n_out % tn == 0 and w_col0 % tn == 0
    c0 = w_col0 // tn
    in_specs = [pl.BlockSpec((tm, k), functools.partial(lambda i, j, cb: (i, cb), cb=cb), pipeline_mode=A_RESIDENT)
                for k, cb in zip(ks, a_col_blk)]
    in_specs.append(pl.BlockSpec((kt, tn), lambda i, j: (w_row_blk, c0 + j)))
    args = arrs + [w]
    if resid is not None:
        in_specs += [pl.BlockSpec((tm, tn), lambda i, j: (i, j)),
                     pl.BlockSpec((2, tn), lambda i, j: (0, j))]
        args += [resid, gate]
    return pl.pallas_call(
        functools.partial(_mm_kernel, ks=ks, act=act, has_res=resid is not None, tm=tm, n_lat=n_lat),
        grid=(m // tm, n_out // tn),
        in_specs=in_specs,
        out_specs=pl.BlockSpec((tm, tn), lambda i, j: (i, j)),
        out_shape=jax.ShapeDtypeStruct((m, n_out), out_dtype),
        compiler_params=_cparams(("arbitrary", "arbitrary")),
        name=name,
    )(*args)


def _qkv_kernel(a_ref, w_ref, g_ref, cos_ref, sa_ref, sb_ref, o_ref, *, v_tiles, hd):
    y = jnp.dot(a_ref[...], w_ref[...].astype(BF16), preferred_element_type=F32)
    j = pl.program_id(1)
    is_v = functools.reduce(jnp.logical_or, [j == t for t in v_tiles])

    @pl.when(is_v)
    def _():
        o_ref[...] = y.astype(o_ref.dtype)

    @pl.when(jnp.logical_not(is_v))
    def _():
        cos, sa, sb = cos_ref[...], sa_ref[...], sb_ref[...]
        for h in range(y.shape[1] // hd):
            yh = y[:, h * hd:(h + 1) * hd]
            yh = yh * lax.rsqrt(jnp.mean(yh * yh, axis=-1, keepdims=True) + EPS) * g_ref[:, h * hd:(h + 1) * hd]
            out = yh * cos + pltpu.roll(yh, hd - 1, 1) * sa + pltpu.roll(yh, 1, 1) * sb
            o_ref[:, h * hd:(h + 1) * hd] = out.astype(o_ref.dtype)


def _qkv(h, w_in, gains, cos, sa, sb, v_tiles, tm, tn):
    m, d = h.shape
    n = w_in.shape[1]
    tab = pl.BlockSpec((tm, HEAD_DIM), lambda i, j: (i, 0))
    return pl.pallas_call(
        functools.partial(_qkv_kernel, v_tiles=v_tiles, hd=HEAD_DIM),
        grid=(m // tm, n // tn),
        in_specs=[pl.BlockSpec((tm, d), lambda i, j: (i, 0), pipeline_mode=A_RESIDENT),
                  pl.BlockSpec((d, tn), lambda i, j: (0, j)),
                  pl.BlockSpec((1, tn), lambda i, j: (0, j)),
                  tab, tab, tab],
        out_specs=pl.BlockSpec((tm, tn), lambda i, j: (i, j)),
        out_shape=jax.ShapeDtypeStruct((m, n), BF16),
        compiler_params=_cparams(("arbitrary", "arbitrary")),
        name="qkv_proj",
    )(h, w_in, gains, cos, sa, sb)


def _gateup_kernel(a_ref, wg_ref, wu_ref, o_ref):
    a = a_ref[...]
    g = jnp.dot(a, wg_ref[...].astype(BF16), preferred_element_type=F32)
    u = jnp.dot(a, wu_ref[...].astype(BF16), preferred_element_type=F32)
    o_ref[...] = (g * jax.nn.sigmoid(g) * u).astype(o_ref.dtype)


def _gateup(h, wg, wu, tm, tn):
    m, d = h.shape
    n = wg.shape[1]
    wspec = pl.BlockSpec((d, tn), lambda i, j: (0, j))
    return pl.pallas_call(
        _gateup_kernel,
        grid=(m // tm, n // tn),
        in_specs=[pl.BlockSpec((tm, d), lambda i, j: (i, 0), pipeline_mode=A_RESIDENT), wspec, wspec],
        out_specs=pl.BlockSpec((tm, tn), lambda i, j: (i, j)),
        out_shape=jax.ShapeDtypeStruct((m, n), BF16),
        compiler_params=_cparams(("arbitrary", "arbitrary")),
        name="ffn_gateup",
    )(h, wg, wu)


def _online_softmax_step(s, m_sc, rows):
    m_old = m_sc[rows, :]
    m_new = jnp.maximum(m_old, jnp.max(s, axis=-1, keepdims=True))
    m_sc[rows, :] = m_new
    return jnp.exp2(s - m_new), jnp.exp2(m_old - m_new)


def _gqa_kernel(q_ref, k_ref, v_ref, o_ref, vaug, m_sc, acc_sc, *, tq, tk, n_chunks, n_rep, hd):
    @pl.when(pl.program_id(1) == 0)
    def _():
        vaug[:, :hd] = v_ref[...]
        vaug[:, hd:] = jnp.ones((vaug.shape[0], hd), vaug.dtype)

    q = jnp.concatenate([q_ref[:, h * hd:(h + 1) * hd] for h in range(n_rep)], axis=0)
    m_sc[...] = jnp.full(m_sc.shape, -jnp.inf, F32)
    acc_sc[...] = jnp.zeros(acc_sc.shape, F32)

    def body(c, carry):
        start = pl.multiple_of(c * tk, tk)
        k = k_ref[pl.ds(start, tk), :]
        v = vaug[pl.ds(start, tk), :]
        for h in range(n_rep):
            rows = pl.ds(h * tq, tq)
            s = lax.dot_general(q[h * tq:(h + 1) * tq], k, NT_DIMS, preferred_element_type=F32)
            p, alpha = _online_softmax_step(s, m_sc, rows)
            acc_sc[rows, :] = alpha * acc_sc[rows, :] + jnp.dot(p.astype(BF16), v, preferred_element_type=F32)
        return carry

    lax.fori_loop(0, n_chunks, body, 0, unroll=2 if n_chunks % 2 == 0 else 1)
    acc = acc_sc[...]
    o = acc[:, :hd] / acc[:, hd:]
    for h in range(n_rep):
        o_ref[:, h * hd:(h + 1) * hd] = o[h * tq:(h + 1) * tq].astype(o_ref.dtype)


def _gqa_attention(qkv, *, n_q, q_blk0, n_k, k_blk0, tq, tk, n_kv, n_rep, q_col0, k_col0, v_col0):
    hd = HEAD_DIM
    qw = n_rep * hd
    return pl.pallas_call(
        functools.partial(_gqa_kernel, tq=tq, tk=tk, n_chunks=n_k // tk, n_rep=n_rep, hd=hd),
        grid=(n_kv, n_q // tq),
        in_specs=[pl.BlockSpec((tq, qw), lambda g, i: (q_blk0 + i, q_col0 // qw + g)),
                  pl.BlockSpec((n_k, hd), lambda g, i: (k_blk0, k_col0 // hd + g)),
                  pl.BlockSpec((n_k, hd), lambda g, i: (k_blk0, v_col0 // hd + g))],
        out_specs=pl.BlockSpec((tq, qw), lambda g, i: (i, g)),
        out_shape=jax.ShapeDtypeStruct((n_q, n_kv * qw), BF16),
        scratch_shapes=[pltpu.VMEM((n_k, 2 * hd), BF16),
                        pltpu.VMEM((n_rep * tq, 1), F32),
                        pltpu.VMEM((n_rep * tq, 2 * hd), F32)],
        compiler_params=_cparams(("arbitrary", "arbitrary")),
        name="gqa_attention",
    )(qkv, qkv, qkv)


def _diff_kernel(q_ref, k_ref, v_ref, lam_ref, sub_ref, o_ref, m_sc, l_sc, acc_sc, *, tq, tk, n_chunks, hd,
                 lam_init):
    q1, q2 = q_ref[:, :hd], q_ref[:, hd:]
    m_sc[...] = jnp.full(m_sc.shape, -jnp.inf, F32)
    l_sc[...] = jnp.zeros(l_sc.shape, F32)
    acc_sc[...] = jnp.zeros(acc_sc.shape, F32)

    def body(c, carry):
        start = pl.multiple_of(c * tk, tk)
        k = k_ref[pl.ds(start, tk), :]
        v = v_ref[pl.ds(start, tk), :]
        s = jnp.concatenate([lax.dot_general(q1, k[:, :hd], NT_DIMS, preferred_element_type=F32),
                             lax.dot_general(q2, k[:, hd:], NT_DIMS, preferred_element_type=F32)], axis=0)
        rows = pl.ds(0, 2 * tq)
        p, alpha = _online_softmax_step(s, m_sc, rows)
        l_sc[...] = alpha * l_sc[...] + jnp.sum(p, axis=-1, keepdims=True)
        acc_sc[...] = alpha * acc_sc[...] + jnp.dot(p.astype(BF16), v, preferred_element_type=F32)
        return carry

    lax.fori_loop(0, n_chunks, body, 0, unroll=2 if n_chunks % 2 == 0 else 1)
    o = acc_sc[...] / l_sc[...]
    lp = lam_ref[...]
    lam = (jnp.exp(jnp.sum(lp[0:1] * lp[1:2], axis=-1, keepdims=True))
           - jnp.exp(jnp.sum(lp[2:3] * lp[3:4], axis=-1, keepdims=True)) + lam_init)
    o = o[:tq] - lam * o[tq:]
    o = o * lax.rsqrt(jnp.mean(o * o, axis=-1, keepdims=True) + EPS) * sub_ref[...] * (1.0 - lam_init)
    o_ref[...] = o.astype(o_ref.dtype)


def _diff_attention(qkv, lam_params, subln, *, n_q, q_blk0, n_k, k_blk0, tq, tk, n_heads,
                    q_col0, k_col0, v_col0, lam_init):
    hd = HEAD_DIM
    w2 = 2 * hd
    return pl.pallas_call(
        functools.partial(_diff_kernel, tq=tq, tk=tk, n_chunks=n_k // tk, hd=hd, lam_init=lam_init),
        grid=(n_heads, n_q // tq),
        in_specs=[pl.BlockSpec((tq, w2), lambda h, i: (q_blk0 + i, q_col0 // w2 + h)),
                  pl.BlockSpec((n_k, w2), lambda h, i: (k_blk0, k_col0 // w2 + h)),
                  pl.BlockSpec((n_k, w2), lambda h, i: (k_blk0, v_col0 // w2 + h)),
                  pl.BlockSpec((8, hd), lambda h, i: (0, 0)),
                  pl.BlockSpec((1, w2), lambda h, i: (0, 0))],
        out_specs=pl.BlockSpec((tq, w2), lambda h, i: (i, h)),
        out_shape=jax.ShapeDtypeStruct((n_q, n_heads * w2), BF16),
        scratch_shapes=[pltpu.VMEM((2 * tq, 1), F32), pltpu.VMEM((2 * tq, 1), F32),
                        pltpu.VMEM((2 * tq, w2), F32)],
        compiler_params=_cparams(("arbitrary", "arbitrary")),
        name="diff_attention",
    )(qkv, qkv, qkv, lam_params, subln)


def _scan_chunk(a, b, h0, reverse):
    n = a.shape[0]
    row = lax.broadcasted_iota(jnp.int32, (n, 1), 0)
    d = 1
    while d < n:
        if reverse:
            a_sh, b_sh, keep = pltpu.roll(a, n - d, 0), pltpu.roll(b, n - d, 0), row < n - d
        else:
            a_sh, b_sh, keep = pltpu.roll(a, d, 0), pltpu.roll(b, d, 0), row >= d
        b = jnp.where(keep, a * b_sh + b, b)
        a = jnp.where(keep, a * a_sh, a)
        d *= 2
    h = a * h0 + b
    return h, (h[0:1] if reverse else h[n - 1:n])


def _rglru_kernel(x_ref, gy_ref, cw_ref, cb_ref, gaw_ref, gab_ref, gxw_ref, gxb_ref, lam_ref, o_ref, s_ref,
                  *, n_lat, n_ctx, tc):
    n_rows = n_lat + n_ctx
    cw, cb = cw_ref[...], cb_ref[...]
    z = -lam_ref[...]
    softplus = jnp.maximum(z, 0.0) + jnp.log(1.0 + jnp.exp(-jnp.abs(z)))

    def conv(t0, n, seg0, seg1):
        xs = x_ref[pl.ds(t0, n), :]
        prev = x_ref[pl.ds(pl.multiple_of(jnp.maximum(t0 - 8, 0), 8), 8), :]
        nxt = x_ref[pl.ds(pl.multiple_of(jnp.minimum(t0 + n, n_rows - 8), 8), 8), :]
        prev = jnp.where(t0 > seg0, prev, 0.0)
        nxt = jnp.where(t0 + n < seg1, nxt, 0.0)
        ext = jnp.concatenate([prev, xs, nxt], axis=0)
        return (ext[6:6 + n] * cw[0:1] + ext[7:7 + n] * cw[1:2] + ext[8:8 + n] * cw[2:3]
                + ext[9:9 + n] * cw[3:4] + cb)

    for d in range(2):
        reverse = d == 1
        wa = gaw_ref[d, 0].astype(BF16)
        wx = gxw_ref[d, 0].astype(BF16)
        ba, bx = gab_ref[d:d + 1, :], gxb_ref[d:d + 1, :]
        c8 = -RG_C * softplus[d:d + 1, :]

        def chunk(t0, n, seg0, seg1, h0, reverse=reverse, wa=wa, wx=wx, ba=ba, bx=bx, c8=c8):
            xc = conv(t0, n, seg0, seg1)
            xb = xc.astype(BF16)
            r = jax.nn.sigmoid(jnp.dot(xb, wa, preferred_element_type=F32) + ba)
            i = jax.nn.sigmoid(jnp.dot(xb, wx, preferred_element_type=F32) + bx)
            a = jnp.exp(c8 * r)
            b = jnp.sqrt(1.0 - a * a) * (i * xc)
            return _scan_chunk(a, b, h0, reverse)

        _, h_end = chunk(n_lat, n_ctx, n_lat, n_rows, jnp.zeros((1, cw.shape[1]), F32))
        n_ch = n_lat // tc

        def body(ci, hc, chunk=chunk, reverse=reverse):
            cidx = (n_ch - 1 - ci) if reverse else ci
            t0 = pl.multiple_of(cidx * tc, tc)
            h, hc = chunk(t0, tc, 0, n_lat, hc)
            if reverse:
                s = s_ref[pl.ds(t0, tc), :] + h
                o_ref[pl.ds(t0, tc), :] = (s * gy_ref[pl.ds(t0, tc), :].astype(F32)).astype(o_ref.dtype)
            else:
                s_ref[pl.ds(t0, tc), :] = h
            return hc

        lax.fori_loop(0, n_ch, body, h_end)


def _rglru(xr, gy, conv_w, conv_b, ga_w, ga_b, gx_w, gx_b, lam, n_lat, n_ctx):
    n_rows, c_tot = xr.shape
    nb, bw = ga_w.shape[1], ga_w.shape[2]
    tc = _tile(n_lat, 512, 8)
    col = lambda j: (0, j)
    return pl.pallas_call(
        functools.partial(_rglru_kernel, n_lat=n_lat, n_ctx=n_ctx, tc=tc),
        grid=(nb,),
        in_specs=[pl.BlockSpec((n_rows, bw), col),
                  pl.BlockSpec((n_lat, bw), col),
                  pl.BlockSpec((conv_w.shape[0], bw), col),
                  pl.BlockSpec((1, bw), col),
                  pl.BlockSpec((2, 1, bw, bw), lambda j: (0, j, 0, 0)),
                  pl.BlockSpec((2, bw), col),
                  pl.BlockSpec((2, 1, bw, bw), lambda j: (0, j, 0, 0)),
                  pl.BlockSpec((2, bw), col),
                  pl.BlockSpec((2, bw), col)],
        out_specs=pl.BlockSpec((n_lat, bw), col),
        out_shape=jax.ShapeDtypeStruct((n_lat, c_tot), BF16),
        scratch_shapes=[pltpu.VMEM((n_lat, bw), F32)],
        compiler_params=_cparams(("arbitrary",)),
        name="rglru",
    )(xr, gy, conv_w, conv_b.reshape(1, c_tot), ga_w, ga_b, gx_w, gx_b, lam)


def _row_copy(src_hbm, row, dst, r, sem):
    return pltpu.make_async_copy(src_hbm.at[pl.ds(row, 1)], dst.at[pl.ds(r, 1)], sem)


def _gather_kernel(src_ref, h_hbm, o_ref, buf, sem, *, tm):
    base = pl.program_id(0) * tm

    def start(r, c):
        _row_copy(h_hbm, src_ref[base + r], buf, r, sem).start()
        return c

    def wait(r, c):
        _row_copy(h_hbm, 0, buf, r, sem).wait()
        return c

    lax.fori_loop(0, tm, start, 0)
    lax.fori_loop(0, tm, wait, 0)
    o_ref[...] = buf[...].astype(o_ref.dtype)


def _moe_gather(h, src_tok, n_rows, tm):
    d = h.shape[1]
    return pl.pallas_call(
        functools.partial(_gather_kernel, tm=tm),
        grid_spec=pltpu.PrefetchScalarGridSpec(
            num_scalar_prefetch=1,
            grid=(n_rows // tm,),
            in_specs=[pl.BlockSpec(memory_space=pl.ANY)],
            out_specs=pl.BlockSpec((tm, d), lambda t, src: (t, 0)),
            scratch_shapes=[pltpu.VMEM((tm, d), F32), pltpu.SemaphoreType.DMA(())]),
        out_shape=jax.ShapeDtypeStruct((n_rows, d), BF16),
        compiler_params=_cparams(("arbitrary",)),
        name="moe_gather",
    )(src_tok, h)


def _expert_changed(te_ref, t):
    return jnp.logical_or(t == 0, te_ref[t] != te_ref[jnp.maximum(t - 1, 0)])


def _gmm1_kernel(te_ref, x_ref, wg_ref, wu_ref, o_ref, wgb, wub):
    @pl.when(_expert_changed(te_ref, pl.program_id(1)))
    def _():
        wgb[...] = wg_ref[0].astype(BF16)
        wub[...] = wu_ref[0].astype(BF16)

    x = x_ref[...]
    g = jnp.dot(x, wgb[...], preferred_element_type=F32)
    u = jnp.dot(x, wub[...], preferred_element_type=F32)
    o_ref[...] = (g * jax.nn.sigmoid(g) * u).astype(o_ref.dtype)


def _gmm2_kernel(te_ref, h_ref, wd_ref, rw_ref, o_ref, wdb):
    @pl.when(_expert_changed(te_ref, pl.program_id(1)))
    def _():
        wdb[...] = wd_ref[0].astype(BF16)

    o_ref[...] = jnp.dot(h_ref[...], wdb[...], preferred_element_type=F32) * rw_ref[...]


def _moe_gmm1(tile_e, xs, wg, wu, tm, tn):
    n_rows, d = xs.shape
    ff = wg.shape[2]
    wspec = pl.BlockSpec((1, d, tn), lambda j, t, te: (te[t], 0, j))
    return pl.pallas_call(
        _gmm1_kernel,
        grid_spec=pltpu.PrefetchScalarGridSpec(
            num_scalar_prefetch=1,
            grid=(ff // tn, n_rows // tm),
            in_specs=[pl.BlockSpec((tm, d), lambda j, t, te: (t, 0)), wspec, wspec],
            out_specs=pl.BlockSpec((tm, tn), lambda j, t, te: (t, j)),
            scratch_shapes=[pltpu.VMEM((d, tn), BF16), pltpu.VMEM((d, tn), BF16)]),
        out_shape=jax.ShapeDtypeStruct((n_rows, ff), BF16),
        compiler_params=_cparams(("arbitrary", "arbitrary")),
        name="moe_gateup",
    )(tile_e, xs, wg, wu)


def _moe_gmm2(tile_e, hs, wd, row_w, tm, tn):
    n_rows, ff = hs.shape
    d = wd.shape[2]
    return pl.pallas_call(
        _gmm2_kernel,
        grid_spec=pltpu.PrefetchScalarGridSpec(
            num_scalar_prefetch=1,
            grid=(d // tn, n_rows // tm),
            in_specs=[pl.BlockSpec((tm, ff), lambda j, t, te: (t, 0)),
                      pl.BlockSpec((1, ff, tn), lambda j, t, te: (te[t], 0, j)),
                      pl.BlockSpec((tm, 1), lambda j, t, te: (t, 0))],
            out_specs=pl.BlockSpec((tm, tn), lambda j, t, te: (t, j)),
            scratch_shapes=[pltpu.VMEM((ff, tn), BF16)]),
        out_shape=jax.ShapeDtypeStruct((n_rows, d), F32),
        compiler_params=_cparams(("arbitrary", "arbitrary")),
        name="moe_down",
    )(tile_e, hs, wd, row_w)


def _combine_kernel(dest_ref, x_ref, g_ref, y_hbm, o_ref, b1, b2, sem, *, tt):
    base = pl.program_id(0) * tt

    def start(r, c):
        _row_copy(y_hbm, dest_ref[2 * (base + r)], b1, r, sem).start()
        _row_copy(y_hbm, dest_ref[2 * (base + r) + 1], b2, r, sem).start()
        return c

    def wait(r, c):
        _row_copy(y_hbm, 0, b1, r, sem).wait()
        _row_copy(y_hbm, 0, b2, r, sem).wait()
        return c

    lax.fori_loop(0, tt, start, 0)
    lax.fori_loop(0, tt, wait, 0)
    o_ref[...] = x_ref[...] + g_ref[0:1, :] * (b1[...] + b2[...])


def _moe_combine(dest, x, gate, y, n_tok, tt):
    d = x.shape[1]
    return pl.pallas_call(
        functools.partial(_combine_kernel, tt=tt),
        grid_spec=pltpu.PrefetchScalarGridSpec(
            num_scalar_prefetch=1,
            grid=(n_tok // tt,),
            in_specs=[pl.BlockSpec((tt, d), lambda i, dst: (i, 0)),
                      pl.BlockSpec((2, d), lambda i, dst: (0, 0)),
                      pl.BlockSpec(memory_space=pl.ANY)],
            out_specs=pl.BlockSpec((tt, d), lambda i, dst: (i, 0)),
            scratch_shapes=[pltpu.VMEM((tt, d), F32), pltpu.VMEM((tt, d), F32),
                            pltpu.SemaphoreType.DMA(())]),
        out_shape=jax.ShapeDtypeStruct((n_tok, d), F32),
        compiler_params=_cparams(("arbitrary",)),
        name="moe_combine",
    )(dest, x, gate, y)


def _moe_plan(e_idx, w_top, n_exp, tm):
    n_tok = e_idx.shape[0]
    n_slot = n_tok * TOP_K
    flat_e = e_idx.reshape(n_slot)
    onehot = (flat_e[:, None] == jnp.arange(n_exp, dtype=jnp.int32)[None, :]).astype(jnp.int32)
    csum = jnp.cumsum(onehot, axis=0)
    rank = jnp.sum(csum * onehot, axis=1) - 1
    counts = csum[-1]
    padded = ((counts + tm - 1) // tm) * tm
    ends = jnp.cumsum(padded)
    starts = ends - padded
    dest = (starts[flat_e] + rank).astype(jnp.int32)
    n_tiles = n_slot // tm + n_exp
    n_rows = n_tiles * tm
    tile_start = jnp.arange(n_tiles, dtype=jnp.int32) * tm
    tile_e = jnp.minimum(jnp.sum((tile_start[:, None] >= ends[None, :]).astype(jnp.int32), axis=1), n_exp - 1)
    src_tok = jnp.zeros((n_rows,), jnp.int32).at[dest].set(jnp.arange(n_slot, dtype=jnp.int32) // TOP_K)
    row_w = jnp.zeros((n_rows, 1), F32).at[dest, 0].set(w_top.reshape(n_slot))
    return dest, src_tok, row_w, tile_e, n_rows


def _rope_tables(n_lat, n_ctx):
    half = HEAD_DIM // 2
    inv = ROPE_THETA ** (-jnp.arange(0, half, 2, dtype=F32) / half)
    rows = n_lat // GRID_W
    row = jnp.repeat(jnp.arange(rows, dtype=F32), GRID_W)
    col = jnp.tile(jnp.arange(GRID_W, dtype=F32), rows)
    ang = jnp.concatenate([row[:, None] * inv, col[:, None] * inv], axis=-1)
    cos = jnp.repeat(jnp.cos(ang), 2, axis=-1)
    sin = jnp.repeat(jnp.sin(ang), 2, axis=-1)
    even = (jnp.arange(HEAD_DIM) % 2 == 0)[None, :]
    sa = jnp.where(even, -sin, 0.0)
    sb = jnp.where(even, 0.0, sin)
    ident = jnp.ones((n_ctx, HEAD_DIM), F32)
    zero = jnp.zeros((n_ctx, HEAD_DIM), F32)
    return (jnp.concatenate([cos, ident]), jnp.concatenate([sa, zero]), jnp.concatenate([sb, zero]))


def kernel(x, c, ctx, c_ctx, attn_w_mod, attn_b_mod, attn_norm_mix, attn_norm_ffn, attn_w_in, attn_gqa_q_norm, attn_gqa_k_norm, attn_diff_q_norm, attn_diff_k_norm, attn_diff_lambda_q1, attn_diff_lambda_k1, attn_diff_lambda_q2, attn_diff_lambda_k2, attn_diff_subln, attn_w_out, ffn_w_gate, ffn_w_up, ffn_w_down, rec_w_mod, rec_b_mod, rec_norm_mix, rec_norm_ffn, rec_w_in, rec_conv_w, rec_conv_b, rec_gate_a_w, rec_gate_a_b, rec_gate_x_w, rec_gate_x_b, rec_lru_lambda, rec_w_out, moe_router_w, moe_router_b, moe_w_gate, moe_w_up, moe_w_down):
    assert x.shape[0] == 1 and attn_w_in.shape[0] == 1 and rec_w_in.shape[0] == 1
    n_lat, d = x.shape[1], x.shape[2]
    n_ctx = ctx.shape[1]
    n_rows = n_lat + n_ctx
    hd = HEAD_DIM
    gqa_heads, diff_heads = d // 256, d // 512
    kv_heads = gqa_heads // 4
    n_rep = gqa_heads // kv_heads
    gq_w, gkv_w = gqa_heads * hd, kv_heads * hd
    dq_w, dv_w = diff_heads * 2 * hd, diff_heads * 2 * hd
    col_ka, col_va, col_qb = gq_w, gq_w + gkv_w, gq_w + 2 * gkv_w
    col_kb, col_vb = col_qb + dq_w, col_qb + 2 * dq_w
    d_ff = ffn_w_gate.shape[2]
    n_exp = moe_router_w.shape[2]

    xs = jnp.concatenate([x[0], ctx[0]], axis=0)
    c2 = jnp.zeros((8, d), F32).at[0].set(c[0]).at[1].set(c_ctx)
    tm_big = _tile(n_rows, 2112, 16)

    mods = _modvec(c2, attn_w_mod[0], attn_b_mod[0])[:2].reshape(2, 6, d)
    sh1, sc1, g1, sh2, sc2, g2 = (mods[:, j] for j in range(6))
    h = _normmod(xs, attn_norm_mix[0], sh1, sc1, n_lat)

    scale = hd ** -0.5 * math.log2(math.e)
    tn_qkv = min(2 * hd, gkv_w)
    gains = jnp.concatenate([
        jnp.tile(attn_gqa_q_norm[0] * scale, gqa_heads), jnp.tile(attn_gqa_k_norm[0], kv_heads),
        jnp.ones((gkv_w,), F32),
        jnp.tile((attn_diff_q_norm[0] * scale).reshape(-1), diff_heads),
        jnp.tile(attn_diff_k_norm[0].reshape(-1), diff_heads), jnp.ones((dv_w,), F32)]).reshape(1, -1)
    v_tiles = tuple(range(col_va // tn_qkv, col_qb // tn_qkv)) + tuple(
        range(col_vb // tn_qkv, (col_vb + dv_w) // tn_qkv))
    cos, sa, sb = _rope_tables(n_lat, n_ctx)
    qkv = _qkv(h, attn_w_in[0], gains, cos, sa, sb, v_tiles, tm_big, tn_qkv)

    tq_ctx = n_ctx
    tq_gqa = _tile(n_lat, 256, 16)
    tq_diff = _tile(n_lat, 512, 16)
    tk_lat = _tile(n_rows, 1536, LANES)
    layer = 0
    lam_init = 0.8 - 0.6 * math.exp(-0.3 * layer)
    lam_params = jnp.zeros((8, hd), F32).at[0].set(attn_diff_lambda_q1[0]).at[1].set(attn_diff_lambda_k1[0]) \
        .at[2].set(attn_diff_lambda_q2[0]).at[3].set(attn_diff_lambda_k2[0])
    subln = attn_diff_subln[0].reshape(1, 2 * hd)
    gqa_args = dict(n_kv=kv_heads, n_rep=n_rep, q_col0=0, k_col0=col_ka, v_col0=col_va)
    diff_args = dict(n_heads=diff_heads, q_col0=col_qb, k_col0=col_kb, v_col0=col_vb, lam_init=lam_init)
    lat_args = dict(n_q=n_lat, q_blk0=0, n_k=n_rows, k_blk0=0, tk=tk_lat)
    ctx_args = dict(n_q=n_ctx, q_blk0=n_lat // tq_ctx, n_k=n_ctx, k_blk0=n_lat // n_ctx, tk=n_ctx, tq=tq_ctx)
    oa = jnp.concatenate([_gqa_attention(qkv, tq=tq_gqa, **lat_args, **gqa_args),
                          _gqa_attention(qkv, **ctx_args, **gqa_args)])
    ob = jnp.concatenate([_diff_attention(qkv, lam_params, subln, tq=tq_diff, **lat_args, **diff_args),
                          _diff_attention(qkv, lam_params, subln, **ctx_args, **diff_args)])

    x1 = _mm([(oa, gq_w), (ob, dv_w)], attn_w_out[0], m=n_rows, tm=tm_big, tn=256, n_out=d, out_dtype=F32,
             resid=xs, gate=g1, n_lat=n_lat, name="attn_out")
    h = _normmod(x1, attn_norm_ffn[0], sh2, sc2, n_lat)
    hff = _gateup(h, ffn_w_gate[0], ffn_w_up[0], tm_big, 256)
    k_half = d_ff // 2
    x2 = x1
    for kb in range(2):
        x2 = _mm([(hff, k_half)], ffn_w_down[0], m=n_rows, tm=_tile(n_rows, 1056, 16), tn=256, n_out=d, out_dtype=F32,
                 w_row_blk=kb, a_col_blk=[kb], resid=x2, gate=g2, n_lat=n_lat, name="ffn_down")

    mods = _modvec(c2, rec_w_mod[0], rec_b_mod[0])[:2].reshape(2, 6, d)
    sh1, sc1, g1, sh2, sc2, g2 = (mods[:, j] for j in range(6))
    h = _normmod(x2, rec_norm_mix[0], sh1, sc1, n_lat)
    d_rnn = rec_w_out.shape[1]
    gy = _mm([(h, d)], rec_w_in[0], m=n_rows, tm=tm_big, tn=256, n_out=d_rnn, out_dtype=BF16,
             act="gelu", name="rec_in_y")
    xr = _mm([(h, d)], rec_w_in[0], m=n_rows, tm=tm_big, tn=256, n_out=d_rnn, out_dtype=F32,
             w_col0=d_rnn, name="rec_in_x")
    s = _rglru(xr, gy, rec_conv_w[0], rec_conv_b[0], rec_gate_a_w[0], rec_gate_a_b[0],
               rec_gate_x_w[0], rec_gate_x_b[0], rec_lru_lambda[0], n_lat, n_ctx)
    tm_lat = _tile(n_lat, 2048, 16)
    x3 = _mm([(s, d_rnn)], rec_w_out[0], m=n_lat, tm=tm_lat, tn=256, n_out=d, out_dtype=F32,
             resid=x2, gate=g1, n_lat=n_lat, name="rec_out")

    h4, w_top, e_top = _normmod_router(x3, rec_norm_ffn[0], sh2, sc2, moe_router_w[0], moe_router_b[0], n_lat)
    tm_e = 256
    dest, src_tok, row_w, tile_e, n_sorted = _moe_plan(e_top[:, :TOP_K], w_top[:, :TOP_K], n_exp, tm_e)
    xg = _moe_gather(h4, src_tok, n_sorted, tm_e)
    hs = _moe_gmm1(tile_e, xg, moe_w_gate[0], moe_w_up[0], tm_e, 512)
    ys = _moe_gmm2(tile_e, hs, moe_w_down[0], row_w, tm_e, 512)
    out = _moe_combine(dest, x3, g2, ys, n_lat, 256)
    return out.reshape(1, n_lat, d)
```

```python
import functools
import math

import jax
import jax.numpy as jnp
from jax import lax
from jax.experimental import pallas as pl
from jax.experimental.pallas import tpu as pltpu

F32 = jnp.float32
BF16 = jnp.bfloat16
EPS = 1e-6
ROPE_THETA = 10000.0
GRID_W = 64
HEAD_DIM = 128
RG_C = 8.0
TOP_K = 2
LANES = 128
SUBLANES = 8
VMEM_LIMIT = 56 * 1024 * 1024
NT_DIMS = (((1,), (1,)), ((), ()))


def _cparams(sem, vmem=VMEM_LIMIT):
    return pltpu.CompilerParams(dimension_semantics=sem, vmem_limit_bytes=vmem)


def _tile(n, target, mult):
    best = None
    for t in range(mult, min(n, target) + 1, mult):
        if n % t == 0:
            best = t
    assert best is not None, (n, target, mult)
    return best


def _row_select(row0, tm, n_lat, ref):
    row = row0 + lax.broadcasted_iota(jnp.int32, (tm, 1), 0)
    return jnp.where(row < n_lat, ref[0:1, :], ref[1:2, :])


def _modvec_kernel(c_ref, w_ref, b_ref, o_ref):
    c = c_ref[...]
    a = (c * jax.nn.sigmoid(c)).astype(BF16)
    o_ref[...] = jnp.dot(a, w_ref[...].astype(BF16), preferred_element_type=F32) + b_ref[...]


def _modvec(c2, w_mod, b_mod):
    d, n = w_mod.shape
    tn = _tile(n, 512, LANES)
    return pl.pallas_call(
        _modvec_kernel,
        grid=(n // tn,),
        in_specs=[pl.BlockSpec((8, d), lambda j: (0, 0)),
                  pl.BlockSpec((d, tn), lambda j: (0, j)),
                  pl.BlockSpec((1, tn), lambda j: (0, j))],
        out_specs=pl.BlockSpec((8, tn), lambda j: (0, j)),
        out_shape=jax.ShapeDtypeStruct((8, n), F32),
        compiler_params=_cparams(("arbitrary",)),
        name="modvec",
    )(c2, w_mod, b_mod.reshape(1, n))


def _normmod_kernel(x_ref, g_ref, sh_ref, sc_ref, o_ref, *, tm, n_lat):
    x = x_ref[...]
    y = x * lax.rsqrt(jnp.mean(x * x, axis=-1, keepdims=True) + EPS) * g_ref[...]
    row0 = pl.program_id(0) * tm
    sc = _row_select(row0, tm, n_lat, sc_ref)
    sh = _row_select(row0, tm, n_lat, sh_ref)
    o_ref[...] = (y * (1.0 + sc) + sh).astype(o_ref.dtype)


def _normmod(x, g, sh, sc, n_lat, m=None):
    m = x.shape[0] if m is None else m
    d = x.shape[1]
    tm = _tile(m, 528, 16)
    vec = pl.BlockSpec((1, d), lambda i: (0, 0))
    two = pl.BlockSpec((2, d), lambda i: (0, 0))
    return pl.pallas_call(
        functools.partial(_normmod_kernel, tm=tm, n_lat=n_lat),
        grid=(m // tm,),
        in_specs=[pl.BlockSpec((tm, d), lambda i: (i, 0)), vec, two, two],
        out_specs=pl.BlockSpec((tm, d), lambda i: (i, 0)),
        out_shape=jax.ShapeDtypeStruct((m, d), BF16),
        compiler_params=_cparams(("arbitrary",)),
        name="normmod",
    )(x, g.reshape(1, d), sh, sc)


def _normmod_router_kernel(x_ref, g_ref, sh_ref, sc_ref, rw_ref, rb_ref, h_ref, w_ref, e_ref, *, n_exp):
    x = x_ref[...]
    y = x * lax.rsqrt(jnp.mean(x * x, axis=-1, keepdims=True) + EPS) * g_ref[...]
    h = y * (1.0 + sc_ref[0:1, :]) + sh_ref[0:1, :]
    h_ref[...] = h
    logits = jnp.dot(h, rw_ref[...], preferred_element_type=F32,
                     precision=lax.Precision.HIGHEST) + rb_ref[...]
    lane = lax.broadcasted_iota(jnp.int32, logits.shape, 1)
    lanef = lane.astype(F32)
    neg = jnp.float32(-jnp.inf)
    lg = jnp.where(lane < n_exp, logits, neg)
    m1 = jnp.max(lg, axis=-1, keepdims=True)
    i1 = jnp.min(jnp.where(lg == m1, lanef, float(LANES)), axis=-1, keepdims=True)
    lg2 = jnp.where(lanef == i1, neg, lg)
    m2 = jnp.max(lg2, axis=-1, keepdims=True)
    i2 = jnp.min(jnp.where(lg2 == m2, lanef, float(LANES)), axis=-1, keepdims=True)
    e = jnp.exp(m2 - m1)
    w1 = 1.0 / (1.0 + e)
    w2 = e / (1.0 + e)
    w_ref[...] = jnp.where(lane == 0, w1, jnp.where(lane == 1, w2, 0.0))
    e_ref[...] = jnp.where(lane == 0, i1, jnp.where(lane == 1, i2, 0.0)).astype(jnp.int32)


def _normmod_router(x, g, sh, sc, router_w, router_b, m):
    d = x.shape[1]
    n_exp = router_w.shape[1]
    tm = _tile(m, 256, 8)
    rw = jnp.zeros((d, LANES), F32).at[:, :n_exp].set(router_w)
    rb = jnp.zeros((1, LANES), F32).at[0, :n_exp].set(router_b)
    vec = pl.BlockSpec((1, d), lambda i: (0, 0))
    two = pl.BlockSpec((2, d), lambda i: (0, 0))
    return pl.pallas_call(
        functools.partial(_normmod_router_kernel, n_exp=n_exp),
        grid=(m // tm,),
        in_specs=[pl.BlockSpec((tm, d), lambda i: (i, 0)), vec, two, two,
                  pl.BlockSpec((d, LANES), lambda i: (0, 0)),
                  pl.BlockSpec((1, LANES), lambda i: (0, 0))],
        out_specs=[pl.BlockSpec((tm, d), lambda i: (i, 0)),
                   pl.BlockSpec((tm, LANES), lambda i: (i, 0)),
                   pl.BlockSpec((tm, LANES), lambda i: (i, 0))],
        out_shape=[jax.ShapeDtypeStruct((m, d), F32),
                   jax.ShapeDtypeStruct((m, LANES), F32),
                   jax.ShapeDtypeStruct((m, LANES), jnp.int32)],
        compiler_params=_cparams(("arbitrary",)),
        name="normmod_router",
    )(x, g.reshape(1, d), sh, sc, rw, rb)


def _mm_kernel(*refs, ks, act, has_res, tm, n_lat):
    n_a = len(ks)
    a_refs, w_ref = refs[:n_a], refs[n_a]
    o_ref = refs[-1]
    w = w_ref[...].astype(BF16)
    acc, off = None, 0
    for a_ref, k in zip(a_refs, ks):
        part = jnp.dot(a_ref[...], w[off:off + k], preferred_element_type=F32)
        acc = part if acc is None else acc + part
        off += k
    if act == "gelu":
        acc = jax.nn.gelu(acc, approximate=True)
    if has_res:
        x_ref, gate_ref = refs[n_a + 1], refs[n_a + 2]
        gate = _row_select(pl.program_id(0) * tm, tm, n_lat, gate_ref)
        acc = x_ref[...] + gate * acc
    o_ref[...] = acc.astype(o_ref.dtype)


def _mm(a_list, w, *, m, tm, tn, n_out, out_dtype, w_row_blk=0, w_col0=0, a_col_blk=None,
        act=None, resid=None, gate=None, n_lat=0, name="mm"):
    ks = tuple(k for _, k in a_list)
    arrs = [a for a, _ in a_list]
    kt = sum(ks)
    a_col_blk = a_col_blk or [0] * len(arrs)
    assert m % tm == 0 and n_out % tn == 0 and w_col0 % tn == 0
    c0 = w_col0 // tn
    in_specs = [pl.BlockSpec((tm, k), functools.partial(lambda i, j, cb: (i, cb), cb=cb))
                for k, cb in zip(ks, a_col_blk)]
    in_specs.append(pl.BlockSpec((kt, tn), lambda i, j: (w_row_blk, c0 + j)))
    args = arrs + [w]
    if resid is not None:
        in_specs += [pl.BlockSpec((tm, tn), lambda i, j: (i, j)),
                     pl.BlockSpec((2, tn), lambda i, j: (0, j))]
        args += [resid, gate]
    return pl.pallas_call(
        functools.partial(_mm_kernel, ks=ks, act=act, has_res=resid is not None, tm=tm, n_lat=n_lat),
        grid=(m // tm, n_out // tn),
        in_specs=in_specs,
        out_specs=pl.BlockSpec((tm, tn), lambda i, j: (i, j)),
        out_shape=jax.ShapeDtypeStruct((m, n_out), out_dtype),
        compiler_params=_cparams(("arbitrary", "arbitrary")),
        name=name,
    )(*args)


def _qkv_kernel(a_ref, w_ref, g_ref, cos_ref, sa_ref, sb_ref, o_ref, *, v_tiles, hd):
    y = jnp.dot(a_ref[...], w_ref[...].astype(BF16), preferred_element_type=F32)
    j = pl.program_id(1)
    is_v = functools.reduce(jnp.logical_or, [j == t for t in v_tiles])

    @pl.when(is_v)
    def _():
        o_ref[...] = y.astype(o_ref.dtype)

    @pl.when(jnp.logical_not(is_v))
    def _():
        cos, sa, sb = cos_ref[...], sa_ref[...], sb_ref[...]
        for h in range(y.shape[1] // hd):
            yh = y[:, h * hd:(h + 1) * hd]
            yh = yh * lax.rsqrt(jnp.mean(yh * yh, axis=-1, keepdims=True) + EPS) * g_ref[:, h * hd:(h + 1) * hd]
            out = yh * cos + pltpu.roll(yh, hd - 1, 1) * sa + pltpu.roll(yh, 1, 1) * sb
            o_ref[:, h * hd:(h + 1) * hd] = out.astype(o_ref.dtype)


def _qkv(h, w_in, gains, cos, sa, sb, v_tiles, tm, tn):
    m, d = h.shape
    n = w_in.shape[1]
    tab = pl.BlockSpec((tm, HEAD_DIM), lambda i, j: (i, 0))
    return pl.pallas_call(
        functools.partial(_qkv_kernel, v_tiles=v_tiles, hd=HEAD_DIM),
        grid=(m // tm, n // tn),
        in_specs=[pl.BlockSpec((tm, d), lambda i, j: (i, 0)),
                  pl.BlockSpec((d, tn), lambda i, j: (0, j)),
                  pl.BlockSpec((1, tn), lambda i, j: (0, j)),
                  tab, tab, tab],
        out_specs=pl.BlockSpec((tm, tn), lambda i, j: (i, j)),
        out_shape=jax.ShapeDtypeStruct((m, n), BF16),
        compiler_params=_cparams(("arbitrary", "arbitrary")),
        name="qkv_proj",
    )(h, w_in, gains, cos, sa, sb)


def _gateup_kernel(a_ref, wg_ref, wu_ref, o_ref):
    a = a_ref[...]
    g = jnp.dot(a, wg_ref[...].astype(BF16), preferred_element_type=F32)
    u = jnp.dot(a, wu_ref[...].astype(BF16), preferred_element_type=F32)
    o_ref[...] = (g * jax.nn.sigmoid(g) * u).astype(o_ref.dtype)


def _gateup(h, wg, wu, tm, tn):
    m, d = h.shape
    n = wg.shape[1]
    wspec = pl.BlockSpec((d, tn), lambda i, j: (0, j))
    return pl.pallas_call(
        _gateup_kernel,
        grid=(m // tm, n // tn),
        in_specs=[pl.BlockSpec((tm, d), lambda i, j: (i, 0)), wspec, wspec],
        out_specs=pl.BlockSpec((tm, tn), lambda i, j: (i, j)),
        out_shape=jax.ShapeDtypeStruct((m, n), BF16),
        compiler_params=_cparams(("arbitrary", "arbitrary")),
        name="ffn_gateup",
    )(h, wg, wu)


def _online_softmax_step(s, m_sc, rows):
    m_old = m_sc[rows, :]
    m_new = jnp.maximum(m_old, jnp.max(s, axis=-1, keepdims=True))
    m_sc[rows, :] = m_new
    return jnp.exp2(s - m_new), jnp.exp2(m_old - m_new)


def _chunk_loop(body, n_chunks):
    lax.fori_loop(0, n_chunks, body, 0, unroll=2 if n_chunks % 2 == 0 else 1)


def _gqa_kernel(q_ref, k_ref, v_ref, o_ref, vaug, m_sc, acc_sc, *, tq, tk, n_chunks, n_rep, hd):
    @pl.when(pl.program_id(1) == 0)
    def _():
        vaug[:, :hd] = v_ref[...]
        vaug[:, hd:] = jnp.ones((vaug.shape[0], hd), vaug.dtype)

    q = jnp.concatenate([q_ref[:, h * hd:(h + 1) * hd] for h in range(n_rep)], axis=0)
    m_sc[...] = jnp.full(m_sc.shape, -jnp.inf, F32)
    acc_sc[...] = jnp.zeros(acc_sc.shape, F32)

    def body(c, carry):
        start = pl.multiple_of(c * tk, tk)
        k = k_ref[pl.ds(start, tk), :]
        v = vaug[pl.ds(start, tk), :]
        for h in range(n_rep):
            rows = pl.ds(h * tq, tq)
            s = lax.dot_general(q[h * tq:(h + 1) * tq], k, NT_DIMS, preferred_element_type=F32)
            p, alpha = _online_softmax_step(s, m_sc, rows)
            acc_sc[rows, :] = alpha * acc_sc[rows, :] + jnp.dot(p.astype(BF16), v, preferred_element_type=F32)
        return carry

    _chunk_loop(body, n_chunks)
    acc = acc_sc[...]
    o = acc[:, :hd] / acc[:, hd:]
    for h in range(n_rep):
        o_ref[:, h * hd:(h + 1) * hd] = o[h * tq:(h + 1) * tq].astype(o_ref.dtype)


def _gqa_attention(qkv, *, n_q, q_blk0, n_k, k_blk0, tq, tk, n_kv, n_rep, q_col0, k_col0, v_col0):
    hd = HEAD_DIM
    qw = n_rep * hd
    return pl.pallas_call(
        functools.partial(_gqa_kernel, tq=tq, tk=tk, n_chunks=n_k // tk, n_rep=n_rep, hd=hd),
        grid=(n_kv, n_q // tq),
        in_specs=[pl.BlockSpec((tq, qw), lambda g, i: (q_blk0 + i, q_col0 // qw + g)),
                  pl.BlockSpec((n_k, hd), lambda g, i: (k_blk0, k_col0 // hd + g)),
                  pl.BlockSpec((n_k, hd), lambda g, i: (k_blk0, v_col0 // hd + g))],
        out_specs=pl.BlockSpec((tq, qw), lambda g, i: (i, g)),
        out_shape=jax.ShapeDtypeStruct((n_q, n_kv * qw), BF16),
        scratch_shapes=[pltpu.VMEM((n_k, 2 * hd), BF16),
                        pltpu.VMEM((n_rep * tq, 1), F32),
                        pltpu.VMEM((n_rep * tq, 2 * hd), F32)],
        compiler_params=_cparams(("arbitrary", "arbitrary")),
        name="gqa_attention",
    )(qkv, qkv, qkv)


def _diff_kernel(q_ref, k_ref, v_ref, lam_ref, sub_ref, o_ref, m_sc, l_sc, acc_sc, *, tq, tk, n_chunks, hd,
                 lam_init):
    q1, q2 = q_ref[:, :hd], q_ref[:, hd:]
    m_sc[...] = jnp.full(m_sc.shape, -jnp.inf, F32)
    l_sc[...] = jnp.zeros(l_sc.shape, F32)
    acc_sc[...] = jnp.zeros(acc_sc.shape, F32)

    def body(c, carry):
        start = pl.multiple_of(c * tk, tk)
        k = k_ref[pl.ds(start, tk), :]
        v = v_ref[pl.ds(start, tk), :]
        s = jnp.concatenate([lax.dot_general(q1, k[:, :hd], NT_DIMS, preferred_element_type=F32),
                             lax.dot_general(q2, k[:, hd:], NT_DIMS, preferred_element_type=F32)], axis=0)
        rows = pl.ds(0, 2 * tq)
        p, alpha = _online_softmax_step(s, m_sc, rows)
        l_sc[...] = alpha * l_sc[...] + jnp.sum(p, axis=-1, keepdims=True)
        acc_sc[...] = alpha * acc_sc[...] + jnp.dot(p.astype(BF16), v, preferred_element_type=F32)
        return carry

    _chunk_loop(body, n_chunks)
    o = acc_sc[...] / l_sc[...]
    lp = lam_ref[...]
    lam = (jnp.exp(jnp.sum(lp[0:1] * lp[1:2], axis=-1, keepdims=True))
           - jnp.exp(jnp.sum(lp[2:3] * lp[3:4], axis=-1, keepdims=True)) + lam_init)
    o = o[:tq] - lam * o[tq:]
    o = o * lax.rsqrt(jnp.mean(o * o, axis=-1, keepdims=True) + EPS) * sub_ref[...] * (1.0 - lam_init)
    o_ref[...] = o.astype(o_ref.dtype)


def _diff_attention(qkv, lam_params, subln, *, n_q, q_blk0, n_k, k_blk0, tq, tk, n_heads,
                    q_col0, k_col0, v_col0, lam_init):
    hd = HEAD_DIM
    w2 = 2 * hd
    return pl.pallas_call(
        functools.partial(_diff_kernel, tq=tq, tk=tk, n_chunks=n_k // tk, hd=hd, lam_init=lam_init),
        grid=(n_heads, n_q // tq),
        in_specs=[pl.BlockSpec((tq, w2), lambda h, i: (q_blk0 + i, q_col0 // w2 + h)),
                  pl.BlockSpec((n_k, w2), lambda h, i: (k_blk0, k_col0 // w2 + h)),
                  pl.BlockSpec((n_k, w2), lambda h, i: (k_blk0, v_col0 // w2 + h)),
                  pl.BlockSpec((8, hd), lambda h, i: (0, 0)),
                  pl.BlockSpec((1, w2), lambda h, i: (0, 0))],
        out_specs=pl.BlockSpec((tq, w2), lambda h, i: (i, h)),
        out_shape=jax.ShapeDtypeStruct((n_q, n_heads * w2), BF16),
        scratch_shapes=[pltpu.VMEM((2 * tq, 1), F32), pltpu.VMEM((2 * tq, 1), F32),
                        pltpu.VMEM((2 * tq, w2), F32)],
        compiler_params=_cparams(("arbitrary", "arbitrary")),
        name="diff_attention",
    )(qkv, qkv, qkv, lam_params, subln)


def _compose_scan(a, b, reverse, group):
    n = a.shape[0]
    pos = lax.broadcasted_iota(jnp.int32, (n, 1), 0) & (group - 1)
    d = 1
    while d < group:
        if reverse:
            a_sh, b_sh, keep = pltpu.roll(a, n - d, 0), pltpu.roll(b, n - d, 0), pos < group - d
        else:
            a_sh, b_sh, keep = pltpu.roll(a, d, 0), pltpu.roll(b, d, 0), pos >= d
        b = jnp.where(keep, a * b_sh + b, b)
        a = jnp.where(keep, a * a_sh, a)
        d *= 2
    return a, b


def _compose_scan_sublanes(a, b, reverse):
    pos = lax.broadcasted_iota(jnp.int32, (1, SUBLANES, 1), 1)
    d = 1
    while d < SUBLANES:
        if reverse:
            a_sh, b_sh, keep = pltpu.roll(a, SUBLANES - d, 1), pltpu.roll(b, SUBLANES - d, 1), pos < SUBLANES - d
        else:
            a_sh, b_sh, keep = pltpu.roll(a, d, 1), pltpu.roll(b, d, 1), pos >= d
        b = jnp.where(keep, a * b_sh + b, b)
        a = jnp.where(keep, a * a_sh, a)
        d *= 2
    return a, b


def _scan_chunk(a, b, h0, reverse, a_sc, b_sc):
    n, c = a.shape
    ng = n // SUBLANES
    a, b = _compose_scan_sublanes(a.reshape(ng, SUBLANES, c), b.reshape(ng, SUBLANES, c), reverse)
    a, b = a.reshape(n, c), b.reshape(n, c)
    edge = 0 if reverse else SUBLANES - 1
    ag, bg = [], []
    for lb in range(c // LANES):
        a_sc[lb, 0:n, :] = a[:, lb * LANES:(lb + 1) * LANES]
        b_sc[lb, 0:n, :] = b[:, lb * LANES:(lb + 1) * LANES]
        ag.append(a_sc[lb, pl.ds(edge, ng, stride=SUBLANES), :])
        bg.append(b_sc[lb, pl.ds(edge, ng, stride=SUBLANES), :])
    ag, bg = jnp.concatenate(ag, axis=1), jnp.concatenate(bg, axis=1)
    ag, bg = _compose_scan(ag, bg, reverse, ng)
    hg = ag * h0 + bg
    grow = lax.broadcasted_iota(jnp.int32, (ng, 1), 0)
    if reverse:
        h_in = jnp.where(grow == ng - 1, h0, pltpu.roll(hg, ng - 1, 0))
        carry = hg[0:1]
    else:
        h_in = jnp.where(grow == 0, h0, pltpu.roll(hg, 1, 0))
        carry = hg[ng - 1:ng]
    h_in = jnp.broadcast_to(h_in[:, None, :], (ng, SUBLANES, c)).reshape(n, c)
    return a * h_in + b, carry


def _rglru_kernel(x_ref, gy_ref, cw_ref, cb_ref, gaw_ref, gab_ref, gxw_ref, gxb_ref, lam_ref, o_ref,
                  s_ref, a_sc, b_sc, *, n_lat, n_ctx, tc):
    n_rows = n_lat + n_ctx
    cw, cb = cw_ref[...], cb_ref[...]
    z = -lam_ref[...]
    softplus = jnp.maximum(z, 0.0) + jnp.log(1.0 + jnp.exp(-jnp.abs(z)))

    def conv(t0, n, seg0, seg1):
        xs = x_ref[pl.ds(t0, n), :]
        prev = x_ref[pl.ds(pl.multiple_of(jnp.maximum(t0 - 8, 0), 8), 8), :]
        nxt = x_ref[pl.ds(pl.multiple_of(jnp.minimum(t0 + n, n_rows - 8), 8), 8), :]
        prev = jnp.where(t0 > seg0, prev, 0.0)
        nxt = jnp.where(t0 + n < seg1, nxt, 0.0)
        ext = jnp.concatenate([prev, xs, nxt], axis=0)
        return (ext[6:6 + n] * cw[0:1] + ext[7:7 + n] * cw[1:2] + ext[8:8 + n] * cw[2:3]
                + ext[9:9 + n] * cw[3:4] + cb)

    for d in range(2):
        reverse = d == 1
        wa = gaw_ref[d, 0].astype(BF16)
        wx = gxw_ref[d, 0].astype(BF16)
        ba, bx = gab_ref[d:d + 1, :], gxb_ref[d:d + 1, :]
        c8 = -RG_C * softplus[d:d + 1, :]

        def chunk(t0, n, seg0, seg1, h0, reverse=reverse, wa=wa, wx=wx, ba=ba, bx=bx, c8=c8):
            xc = conv(t0, n, seg0, seg1)
            xb = xc.astype(BF16)
            r = jax.nn.sigmoid(jnp.dot(xb, wa, preferred_element_type=F32) + ba)
            i = jax.nn.sigmoid(jnp.dot(xb, wx, preferred_element_type=F32) + bx)
            a = jnp.exp(c8 * r)
            b = jnp.sqrt(1.0 - a * a) * (i * xc)
            return _scan_chunk(a, b, h0, reverse, a_sc, b_sc)

        _, h_end = chunk(n_lat, n_ctx, n_lat, n_rows, jnp.zeros((1, cw.shape[1]), F32))
        n_ch = n_lat // tc

        def body(ci, hc, chunk=chunk, reverse=reverse):
            cidx = (n_ch - 1 - ci) if reverse else ci
            t0 = pl.multiple_of(cidx * tc, tc)
            h, hc = chunk(t0, tc, 0, n_lat, hc)
            if reverse:
                s = s_ref[pl.ds(t0, tc), :] + h
                o_ref[pl.ds(t0, tc), :] = (s * gy_ref[pl.ds(t0, tc), :].astype(F32)).astype(o_ref.dtype)
            else:
                s_ref[pl.ds(t0, tc), :] = h
            return hc

        lax.fori_loop(0, n_ch, body, h_end)


def _rglru(xr, gy, conv_w, conv_b, ga_w, ga_b, gx_w, gx_b, lam, n_lat, n_ctx):
    n_rows, c_tot = xr.shape
    nb, bw = ga_w.shape[1], ga_w.shape[2]
    tc = _tile(n_lat, 512, 64)
    assert n_ctx % 64 == 0
    col = lambda j: (0, j)
    return pl.pallas_call(
        functools.partial(_rglru_kernel, n_lat=n_lat, n_ctx=n_ctx, tc=tc),
        grid=(nb,),
        in_specs=[pl.BlockSpec((n_rows, bw), col),
                  pl.BlockSpec((n_lat, bw), col),
                  pl.BlockSpec((conv_w.shape[0], bw), col),
                  pl.BlockSpec((1, bw), col),
                  pl.BlockSpec((2, 1, bw, bw), lambda j: (0, j, 0, 0)),
                  pl.BlockSpec((2, bw), col),
                  pl.BlockSpec((2, 1, bw, bw), lambda j: (0, j, 0, 0)),
                  pl.BlockSpec((2, bw), col),
                  pl.BlockSpec((2, bw), col)],
        out_specs=pl.BlockSpec((n_lat, bw), col),
        out_shape=jax.ShapeDtypeStruct((n_lat, c_tot), BF16),
        scratch_shapes=[pltpu.VMEM((n_lat, bw), F32),
                        pltpu.VMEM((bw // LANES, max(tc, n_ctx), LANES), F32),
                        pltpu.VMEM((bw // LANES, max(tc, n_ctx), LANES), F32)],
        compiler_params=_cparams(("arbitrary",)),
        name="rglru",
    )(xr, gy, conv_w, conv_b.reshape(1, c_tot), ga_w, ga_b, gx_w, gx_b, lam)


def _row_copy(src_hbm, row, dst, r, sem):
    return pltpu.make_async_copy(src_hbm.at[pl.ds(row, 1)], dst.at[pl.ds(r, 1)], sem)


def _gather_kernel(src_ref, nu_ref, h_hbm, o_ref, buf, sem, *, tm):
    t = pl.program_id(0)
    n_used = nu_ref[0]

    def issue(tile, slot):
        def start(r, c):
            _row_copy(h_hbm, src_ref[tile * tm + r], buf.at[slot], r, sem.at[slot]).start()
            return c
        lax.fori_loop(0, tm, start, 0)

    @pl.when(t == 0)
    def _():
        issue(0, 0)

    @pl.when(t + 1 < n_used)
    def _():
        issue(t + 1, (t + 1) % 2)

    @pl.when(t < n_used)
    def _():
        slot = t % 2

        def wait(r, c):
            _row_copy(h_hbm, 0, buf.at[slot], r, sem.at[slot]).wait()
            return c
        lax.fori_loop(0, tm, wait, 0)
        o_ref[...] = buf[slot].astype(o_ref.dtype)

    @pl.when(t >= n_used)
    def _():
        o_ref[...] = jnp.zeros(o_ref.shape, o_ref.dtype)


def _moe_gather(h, src_tok, n_used, n_rows, tm):
    d = h.shape[1]
    return pl.pallas_call(
        functools.partial(_gather_kernel, tm=tm),
        grid_spec=pltpu.PrefetchScalarGridSpec(
            num_scalar_prefetch=2,
            grid=(n_rows // tm,),
            in_specs=[pl.BlockSpec(memory_space=pl.ANY)],
            out_specs=pl.BlockSpec((tm, d), lambda t, src, nu: (t, 0)),
            scratch_shapes=[pltpu.VMEM((2, tm, d), F32), pltpu.SemaphoreType.DMA((2,))]),
        out_shape=jax.ShapeDtypeStruct((n_rows, d), BF16),
        compiler_params=_cparams(("arbitrary",)),
        name="moe_gather",
    )(src_tok, n_used, h)


def _expert_rows_kernel(ts_ref, nt_ref, nu_ref, x_hbm, *refs, n_w, tm, tn, compute):
    w_refs, o_hbm = refs[:n_w], refs[n_w]
    wb = refs[n_w + 1:2 * n_w + 1]
    xbuf, obuf, sem_in, sem_out = refs[2 * n_w + 1:]
    j, e = pl.program_id(0), pl.program_id(1)
    nt = nt_ref[e]
    row0 = ts_ref[e] * tm
    col0 = pl.multiple_of(j * tn, tn)

    def in_copy(t, slot):
        r = pl.multiple_of(row0 + t * tm, tm)
        return pltpu.make_async_copy(x_hbm.at[pl.ds(r, tm)], xbuf.at[slot], sem_in.at[slot])

    def out_copy(t, slot):
        r = pl.multiple_of(row0 + t * tm, tm)
        return pltpu.make_async_copy(obuf.at[slot], o_hbm.at[pl.ds(r, tm), pl.ds(col0, tn)], sem_out.at[slot])

    @pl.when(nt > 0)
    def _():
        in_copy(0, 0).start()
        for w_ref, w_b in zip(w_refs, wb):
            w_b[...] = w_ref[0].astype(BF16)

        def body(t, c):
            slot = t % 2

            @pl.when(t + 1 < nt)
            def _():
                in_copy(t + 1, 1 - slot).start()

            in_copy(t, slot).wait()

            @pl.when(t >= 2)
            def _():
                out_copy(t - 2, slot).wait()

            obuf[slot] = compute(xbuf[slot], *[w_b[...] for w_b in wb]).astype(obuf.dtype)
            out_copy(t, slot).start()
            return c

        lax.fori_loop(0, nt, body, 0)

        @pl.when(nt >= 2)
        def _():
            out_copy(nt - 2, nt % 2).wait()
        out_copy(nt - 1, (nt - 1) % 2).wait()

    @pl.when(e == pl.num_programs(1) - 1)
    def _():
        obuf[0] = jnp.zeros(obuf.shape[1:], obuf.dtype)

        def zero_tile(t, c):
            cp = pltpu.make_async_copy(obuf.at[0], o_hbm.at[pl.ds(pl.multiple_of(t * tm, tm), tm), pl.ds(col0, tn)],
                                       sem_out.at[0])
            cp.start()
            cp.wait()
            return c

        lax.fori_loop(nu_ref[0], o_hbm.shape[0] // tm, zero_tile, 0)


def _swiglu_tile(x, wg, wu):
    g = jnp.dot(x, wg, preferred_element_type=F32)
    u = jnp.dot(x, wu, preferred_element_type=F32)
    return g * jax.nn.sigmoid(g) * u


def _down_tile(x, wd):
    return jnp.dot(x, wd, preferred_element_type=F32)


def _moe_expert_matmul(tile_start, n_tiles, n_used, xs, weights, compute, out_dtype, tm, tn, name):
    n_rows, k = xs.shape
    n_exp, _, n_out = weights[0].shape
    n_w = len(weights)
    wspec = pl.BlockSpec((1, k, tn), lambda j, e, ts, nt, nu: (e, 0, j))
    any_spec = pl.BlockSpec(memory_space=pl.ANY)
    return pl.pallas_call(
        functools.partial(_expert_rows_kernel, n_w=n_w, tm=tm, tn=tn, compute=compute),
        grid_spec=pltpu.PrefetchScalarGridSpec(
            num_scalar_prefetch=3,
            grid=(n_out // tn, n_exp),
            in_specs=[any_spec] + [wspec] * n_w,
            out_specs=any_spec,
            scratch_shapes=[pltpu.VMEM((k, tn), BF16)] * n_w + [
                pltpu.VMEM((2, tm, k), xs.dtype), pltpu.VMEM((2, tm, tn), out_dtype),
                pltpu.SemaphoreType.DMA((2,)), pltpu.SemaphoreType.DMA((2,))]),
        out_shape=jax.ShapeDtypeStruct((n_rows, n_out), out_dtype),
        compiler_params=_cparams(("arbitrary", "arbitrary")),
        name=name,
    )(tile_start, n_tiles, n_used, xs, *weights)


def _combine_kernel(dest_ref, x_ref, g_ref, w_ref, y_hbm, o_ref, buf, sem, *, tt):
    i = pl.program_id(0)

    def issue(tile, slot):
        def start(r, c):
            tok = tile * tt + r
            for kk in range(TOP_K):
                _row_copy(y_hbm, dest_ref[TOP_K * tok + kk], buf.at[slot, kk], r, sem.at[slot]).start()
            return c
        lax.fori_loop(0, tt, start, 0)

    @pl.when(i == 0)
    def _():
        issue(0, 0)

    @pl.when(i + 1 < pl.num_programs(0))
    def _():
        issue(i + 1, (i + 1) % 2)

    slot = i % 2

    def wait(r, c):
        for kk in range(TOP_K):
            _row_copy(y_hbm, 0, buf.at[slot, kk], r, sem.at[slot]).wait()
        return c
    lax.fori_loop(0, tt, wait, 0)
    w = w_ref[...]
    mix = w[:, 0:1] * buf[slot, 0] + w[:, 1:2] * buf[slot, 1]
    o_ref[...] = x_ref[...] + g_ref[0:1, :] * mix


def _moe_combine(dest, x, gate, w_top, y, n_tok, tt):
    d = x.shape[1]
    return pl.pallas_call(
        functools.partial(_combine_kernel, tt=tt),
        grid_spec=pltpu.PrefetchScalarGridSpec(
            num_scalar_prefetch=1,
            grid=(n_tok // tt,),
            in_specs=[pl.BlockSpec((tt, d), lambda i, dst: (i, 0)),
                      pl.BlockSpec((2, d), lambda i, dst: (0, 0)),
                      pl.BlockSpec((tt, LANES), lambda i, dst: (i, 0)),
                      pl.BlockSpec(memory_space=pl.ANY)],
            out_specs=pl.BlockSpec((tt, d), lambda i, dst: (i, 0)),
            scratch_shapes=[pltpu.VMEM((2, TOP_K, tt, d), F32), pltpu.SemaphoreType.DMA((2,))]),
        out_shape=jax.ShapeDtypeStruct((n_tok, d), F32),
        compiler_params=_cparams(("arbitrary",)),
        name="moe_combine",
    )(dest, x, gate, w_top, y)


def _moe_plan(e_idx, n_exp, tm):
    n_tok = e_idx.shape[0]
    n_slot = n_tok * TOP_K
    flat_e = e_idx.reshape(n_slot)
    onehot = (flat_e[:, None] == jnp.arange(n_exp, dtype=jnp.int32)[None, :]).astype(jnp.int32)
    csum = jnp.cumsum(onehot, axis=0)
    rank = jnp.sum(csum * onehot, axis=1) - 1
    counts = csum[-1]
    n_tiles = (counts + tm - 1) // tm
    tile_end = jnp.cumsum(n_tiles)
    tile_start = tile_end - n_tiles
    dest = (tile_start[flat_e] * tm + rank).astype(jnp.int32)
    n_rows = (n_slot // tm + n_exp) * tm
    src_tok = jnp.zeros((n_rows,), jnp.int32).at[dest].set(jnp.arange(n_slot, dtype=jnp.int32) // TOP_K)
    return dest, src_tok, tile_start.astype(jnp.int32), n_tiles.astype(jnp.int32), tile_end[-1:].astype(jnp.int32), n_rows


def _rope_tables(n_lat, n_ctx):
    half = HEAD_DIM // 2
    inv = ROPE_THETA ** (-jnp.arange(0, half, 2, dtype=F32) / half)
    rows = n_lat // GRID_W
    row = jnp.repeat(jnp.arange(rows, dtype=F32), GRID_W)
    col = jnp.tile(jnp.arange(GRID_W, dtype=F32), rows)
    ang = jnp.concatenate([row[:, None] * inv, col[:, None] * inv], axis=-1)
    cos = jnp.repeat(jnp.cos(ang), 2, axis=-1)
    sin = jnp.repeat(jnp.sin(ang), 2, axis=-1)
    even = (jnp.arange(HEAD_DIM) % 2 == 0)[None, :]
    sa = jnp.where(even, -sin, 0.0)
    sb = jnp.where(even, 0.0, sin)
    ident = jnp.ones((n_ctx, HEAD_DIM), F32)
    zero = jnp.zeros((n_ctx, HEAD_DIM), F32)
    return (jnp.concatenate([cos, ident]), jnp.concatenate([sa, zero]), jnp.concatenate([sb, zero]))


def kernel(x, c, ctx, c_ctx, attn_w_mod, attn_b_mod, attn_norm_mix, attn_norm_ffn, attn_w_in, attn_gqa_q_norm, attn_gqa_k_norm, attn_diff_q_norm, attn_diff_k_norm, attn_diff_lambda_q1, attn_diff_lambda_k1, attn_diff_lambda_q2, attn_diff_lambda_k2, attn_diff_subln, attn_w_out, ffn_w_gate, ffn_w_up, ffn_w_down, rec_w_mod, rec_b_mod, rec_norm_mix, rec_norm_ffn, rec_w_in, rec_conv_w, rec_conv_b, rec_gate_a_w, rec_gate_a_b, rec_gate_x_w, rec_gate_x_b, rec_lru_lambda, rec_w_out, moe_router_w, moe_router_b, moe_w_gate, moe_w_up, moe_w_down):
    assert x.shape[0] == 1 and attn_w_in.shape[0] == 1 and rec_w_in.shape[0] == 1
    n_lat, d = x.shape[1], x.shape[2]
    n_ctx = ctx.shape[1]
    n_rows = n_lat + n_ctx
    hd = HEAD_DIM
    gqa_heads, diff_heads = d // 256, d // 512
    kv_heads = gqa_heads // 4
    n_rep = gqa_heads // kv_heads
    gq_w, gkv_w = gqa_heads * hd, kv_heads * hd
    dq_w, dv_w = diff_heads * 2 * hd, diff_heads * 2 * hd
    col_ka, col_va, col_qb = gq_w, gq_w + gkv_w, gq_w + 2 * gkv_w
    col_kb, col_vb = col_qb + dq_w, col_qb + 2 * dq_w
    d_ff = ffn_w_gate.shape[2]
    n_exp = moe_router_w.shape[2]

    xs = jnp.concatenate([x[0], ctx[0]], axis=0)
    c2 = jnp.zeros((8, d), F32).at[0].set(c[0]).at[1].set(c_ctx)
    tm_big = _tile(n_rows, 1056, 16)

    mods = _modvec(c2, attn_w_mod[0], attn_b_mod[0])[:2].reshape(2, 6, d)
    sh1, sc1, g1, sh2, sc2, g2 = (mods[:, j] for j in range(6))
    h = _normmod(xs, attn_norm_mix[0], sh1, sc1, n_lat)

    scale = hd ** -0.5 * math.log2(math.e)
    tn_qkv = min(4 * hd, gkv_w)
    gains = jnp.concatenate([
        jnp.tile(attn_gqa_q_norm[0] * scale, gqa_heads), jnp.tile(attn_gqa_k_norm[0], kv_heads),
        jnp.ones((gkv_w,), F32),
        jnp.tile((attn_diff_q_norm[0] * scale).reshape(-1), diff_heads),
        jnp.tile(attn_diff_k_norm[0].reshape(-1), diff_heads), jnp.ones((dv_w,), F32)]).reshape(1, -1)
    v_tiles = tuple(range(col_va // tn_qkv, col_qb // tn_qkv)) + tuple(
        range(col_vb // tn_qkv, (col_vb + dv_w) // tn_qkv))
    cos, sa, sb = _rope_tables(n_lat, n_ctx)
    qkv = _qkv(h, attn_w_in[0], gains, cos, sa, sb, v_tiles, tm_big, tn_qkv)

    tq_ctx = n_ctx
    tq_gqa = _tile(n_lat, 256, 16)
    tq_diff = _tile(n_lat, 512, 16)
    tk_lat = _tile(n_rows, 1536, LANES)
    layer = 0
    lam_init = 0.8 - 0.6 * math.exp(-0.3 * layer)
    lam_params = jnp.zeros((8, hd), F32).at[0].set(attn_diff_lambda_q1[0]).at[1].set(attn_diff_lambda_k1[0]) \
        .at[2].set(attn_diff_lambda_q2[0]).at[3].set(attn_diff_lambda_k2[0])
    subln = attn_diff_subln[0].reshape(1, 2 * hd)
    gqa_args = dict(n_kv=kv_heads, n_rep=n_rep, q_col0=0, k_col0=col_ka, v_col0=col_va)
    diff_args = dict(n_heads=diff_heads, q_col0=col_qb, k_col0=col_kb, v_col0=col_vb, lam_init=lam_init)
    lat_args = dict(n_q=n_lat, q_blk0=0, n_k=n_rows, k_blk0=0, tk=tk_lat)
    ctx_args = dict(n_q=n_ctx, q_blk0=n_lat // tq_ctx, n_k=n_ctx, k_blk0=n_lat // n_ctx, tk=n_ctx, tq=tq_ctx)
    oa = jnp.concatenate([_gqa_attention(qkv, tq=tq_gqa, **lat_args, **gqa_args),
                          _gqa_attention(qkv, **ctx_args, **gqa_args)])
    ob = jnp.concatenate([_diff_attention(qkv, lam_params, subln, tq=tq_diff, **lat_args, **diff_args),
                          _diff_attention(qkv, lam_params, subln, **ctx_args, **diff_args)])

    x1 = _mm([(oa, gq_w), (ob, dv_w)], attn_w_out[0], m=n_rows, tm=tm_big, tn=512, n_out=d, out_dtype=F32,
             resid=xs, gate=g1, n_lat=n_lat, name="attn_out")
    h = _normmod(x1, attn_norm_ffn[0], sh2, sc2, n_lat)
    hff = _gateup(h, ffn_w_gate[0], ffn_w_up[0], tm_big, 256)
    k_half = d_ff // 2
    x2 = x1
    for kb in range(2):
        x2 = _mm([(hff, k_half)], ffn_w_down[0], m=n_rows, tm=tm_big, tn=256, n_out=d, out_dtype=F32,
                 w_row_blk=kb, a_col_blk=[kb], resid=x2, gate=g2, n_lat=n_lat, name="ffn_down")

    mods = _modvec(c2, rec_w_mod[0], rec_b_mod[0])[:2].reshape(2, 6, d)
    sh1, sc1, g1, sh2, sc2, g2 = (mods[:, j] for j in range(6))
    h = _normmod(x2, rec_norm_mix[0], sh1, sc1, n_lat)
    d_rnn = rec_w_out.shape[1]
    gy = _mm([(h, d)], rec_w_in[0], m=n_rows, tm=tm_big, tn=512, n_out=d_rnn, out_dtype=BF16,
             act="gelu", name="rec_in_y")
    xr = _mm([(h, d)], rec_w_in[0], m=n_rows, tm=tm_big, tn=512, n_out=d_rnn, out_dtype=F32,
             w_col0=d_rnn, name="rec_in_x")
    s = _rglru(xr, gy, rec_conv_w[0], rec_conv_b[0], rec_gate_a_w[0], rec_gate_a_b[0],
               rec_gate_x_w[0], rec_gate_x_b[0], rec_lru_lambda[0], n_lat, n_ctx)
    tm_lat = _tile(n_lat, 1024, 16)
    x3 = _mm([(s, d_rnn)], rec_w_out[0], m=n_lat, tm=tm_lat, tn=512, n_out=d, out_dtype=F32,
             resid=x2, gate=g1, n_lat=n_lat, name="rec_out")

    h4, w_top, e_top = _normmod_router(x3, rec_norm_ffn[0], sh2, sc2, moe_router_w[0], moe_router_b[0], n_lat)
    tm_e = 256
    dest, src_tok, tile_start, n_tiles, n_used, n_sorted = _moe_plan(e_top[:, :TOP_K], n_exp, tm_e)
    xg = _moe_gather(h4, src_tok, n_used, n_sorted, tm_e)
    hs = _moe_expert_matmul(tile_start, n_tiles, n_used, xg, [moe_w_gate[0], moe_w_up[0]], _swiglu_tile, BF16,
                            tm_e, 512, "moe_gateup")
    ys = _moe_expert_matmul(tile_start, n_tiles, n_used, hs, [moe_w_down[0]], _down_tile, F32, tm_e, 512,
                            "moe_down")
    out = _moe_combine(dest, x3, g2, w_top, ys, n_lat, 256)
    return out.reshape(1, n_lat, d)
```

```python
import functools
import math

import jax
import jax.numpy as jnp
from jax import lax
from jax.experimental import pallas as pl
from jax.experimental.pallas import tpu as pltpu

F32 = jnp.float32
BF16 = jnp.bfloat16
EPS = 1e-6
ROPE_THETA = 10000.0
GRID_W = 64
HEAD_DIM = 128
RG_C = 8.0
TOP_K = 2
LANES = 128
SUBLANES = 8
VMEM_LIMIT = 56 * 1024 * 1024
NT_DIMS = (((1,), (1,)), ((), ()))


def _cparams(sem, vmem=VMEM_LIMIT):
    return pltpu.CompilerParams(dimension_semantics=sem, vmem_limit_bytes=vmem)


def _tile(n, target, mult):
    best = None
    for t in range(mult, min(n, target) + 1, mult):
        if n % t == 0:
            best = t
    assert best is not None, (n, target, mult)
    return best


def _row_select(row0, tm, n_lat, ref):
    row = row0 + lax.broadcasted_iota(jnp.int32, (tm, 1), 0)
    return jnp.where(row < n_lat, ref[0:1, :], ref[1:2, :])


def _modvec_kernel(c_ref, w_ref, b_ref, o_ref):
    c = c_ref[...]
    a = (c * jax.nn.sigmoid(c)).astype(BF16)
    o_ref[...] = jnp.dot(a, w_ref[...].astype(BF16), preferred_element_type=F32) + b_ref[...]


def _modvec(c2, w_mod, b_mod):
    d, n = w_mod.shape
    tn = _tile(n, 512, LANES)
    return pl.pallas_call(
        _modvec_kernel,
        grid=(n // tn,),
        in_specs=[pl.BlockSpec((8, d), lambda j: (0, 0)),
                  pl.BlockSpec((d, tn), lambda j: (0, j)),
                  pl.BlockSpec((1, tn), lambda j: (0, j))],
        out_specs=pl.BlockSpec((8, tn), lambda j: (0, j)),
        out_shape=jax.ShapeDtypeStruct((8, n), F32),
        compiler_params=_cparams(("arbitrary",)),
        name="modvec",
    )(c2, w_mod, b_mod.reshape(1, n))


def _normmod_kernel(x_ref, g_ref, sh_ref, sc_ref, o_ref, *, tm, n_lat):
    x = x_ref[...]
    y = x * lax.rsqrt(jnp.mean(x * x, axis=-1, keepdims=True) + EPS) * g_ref[...]
    row0 = pl.program_id(0) * tm
    sc = _row_select(row0, tm, n_lat, sc_ref)
    sh = _row_select(row0, tm, n_lat, sh_ref)
    o_ref[...] = (y * (1.0 + sc) + sh).astype(o_ref.dtype)


def _normmod(x, g, sh, sc, n_lat, m=None):
    m = x.shape[0] if m is None else m
    d = x.shape[1]
    tm = _tile(m, 528, 16)
    vec = pl.BlockSpec((1, d), lambda i: (0, 0))
    two = pl.BlockSpec((2, d), lambda i: (0, 0))
    return pl.pallas_call(
        functools.partial(_normmod_kernel, tm=tm, n_lat=n_lat),
        grid=(m // tm,),
        in_specs=[pl.BlockSpec((tm, d), lambda i: (i, 0)), vec, two, two],
        out_specs=pl.BlockSpec((tm, d), lambda i: (i, 0)),
        out_shape=jax.ShapeDtypeStruct((m, d), BF16),
        compiler_params=_cparams(("arbitrary",)),
        name="normmod",
    )(x, g.reshape(1, d), sh, sc)


def _normmod_router_kernel(x_ref, g_ref, sh_ref, sc_ref, rw_ref, rb_ref, h_ref, w_ref, e_ref, *, n_exp):
    x = x_ref[...]
    y = x * lax.rsqrt(jnp.mean(x * x, axis=-1, keepdims=True) + EPS) * g_ref[...]
    h = y * (1.0 + sc_ref[0:1, :]) + sh_ref[0:1, :]
    h_ref[...] = h
    logits = jnp.dot(h, rw_ref[...], preferred_element_type=F32,
                     precision=lax.Precision.HIGHEST) + rb_ref[...]
    lane = lax.broadcasted_iota(jnp.int32, logits.shape, 1)
    lanef = lane.astype(F32)
    neg = jnp.float32(-jnp.inf)
    lg = jnp.where(lane < n_exp, logits, neg)
    m1 = jnp.max(lg, axis=-1, keepdims=True)
    i1 = jnp.min(jnp.where(lg == m1, lanef, float(LANES)), axis=-1, keepdims=True)
    lg2 = jnp.where(lanef == i1, neg, lg)
    m2 = jnp.max(lg2, axis=-1, keepdims=True)
    i2 = jnp.min(jnp.where(lg2 == m2, lanef, float(LANES)), axis=-1, keepdims=True)
    e = jnp.exp(m2 - m1)
    w1 = 1.0 / (1.0 + e)
    w2 = e / (1.0 + e)
    w_ref[...] = jnp.where(lane == 0, w1, jnp.where(lane == 1, w2, 0.0))
    e_ref[...] = jnp.where(lane == 0, i1, jnp.where(lane == 1, i2, 0.0)).astype(jnp.int32)


def _normmod_router(x, g, sh, sc, router_w, router_b, m):
    d = x.shape[1]
    n_exp = router_w.shape[1]
    tm = _tile(m, 256, 8)
    rw = jnp.zeros((d, LANES), F32).at[:, :n_exp].set(router_w)
    rb = jnp.zeros((1, LANES), F32).at[0, :n_exp].set(router_b)
    vec = pl.BlockSpec((1, d), lambda i: (0, 0))
    two = pl.BlockSpec((2, d), lambda i: (0, 0))
    return pl.pallas_call(
        functools.partial(_normmod_router_kernel, n_exp=n_exp),
        grid=(m // tm,),
        in_specs=[pl.BlockSpec((tm, d), lambda i: (i, 0)), vec, two, two,
                  pl.BlockSpec((d, LANES), lambda i: (0, 0)),
                  pl.BlockSpec((1, LANES), lambda i: (0, 0))],
        out_specs=[pl.BlockSpec((tm, d), lambda i: (i, 0)),
                   pl.BlockSpec((tm, LANES), lambda i: (i, 0)),
                   pl.BlockSpec((tm, LANES), lambda i: (i, 0))],
        out_shape=[jax.ShapeDtypeStruct((m, d), F32),
                   jax.ShapeDtypeStruct((m, LANES), F32),
                   jax.ShapeDtypeStruct((m, LANES), jnp.int32)],
        compiler_params=_cparams(("arbitrary",)),
        name="normmod_router",
    )(x, g.reshape(1, d), sh, sc, rw, rb)


def _mm_kernel(*refs, ks, act, has_res, tm, n_lat):
    n_a = len(ks)
    a_refs, w_ref = refs[:n_a], refs[n_a]
    o_ref = refs[-1]
    w = w_ref[...].astype(BF16)
    acc, off = None, 0
    for a_ref, k in zip(a_refs, ks):
        part = jnp.dot(a_ref[...], w[off:off + k], preferred_element_type=F32)
        acc = part if acc is None else acc + part
        off += k
    if act == "gelu":
        acc = jax.nn.gelu(acc, approximate=True)
    if has_res:
        x_ref, gate_ref = refs[n_a + 1], refs[n_a + 2]
        gate = _row_select(pl.program_id(0) * tm, tm, n_lat, gate_ref)
        acc = x_ref[...] + gate * acc
    o_ref[...] = acc.astype(o_ref.dtype)


def _mm(a_list, w, *, m, tm, tn, n_out, out_dtype, w_row_blk=0, w_col0=0, a_col_blk=None,
        act=None, resid=None, gate=None, n_lat=0, name="mm"):
    ks = tuple(k for _, k in a_list)
    arrs = [a for a, _ in a_list]
    kt = sum(ks)
    a_col_blk = a_col_blk or [0] * len(arrs)
    assert m % tm == 0 and n_out % tn == 0 and w_col0 % tn == 0
    c0 = w_col0 // tn
    in_specs = [pl.BlockSpec((tm, k), functools.partial(lambda i, j, cb: (i, cb), cb=cb))
                for k, cb in zip(ks, a_col_blk)]
    in_specs.append(pl.BlockSpec((kt, tn), lambda i, j: (w_row_blk, c0 + j)))
    args = arrs + [w]
    if resid is not None:
        in_specs += [pl.BlockSpec((tm, tn), lambda i, j: (i, j)),
                     pl.BlockSpec((2, tn), lambda i, j: (0, j))]
        args += [resid, gate]
    return pl.pallas_call(
        functools.partial(_mm_kernel, ks=ks, act=act, has_res=resid is not None, tm=tm, n_lat=n_lat),
        grid=(m // tm, n_out // tn),
        in_specs=in_specs,
        out_specs=pl.BlockSpec((tm, tn), lambda i, j: (i, j)),
        out_shape=jax.ShapeDtypeStruct((m, n_out), out_dtype),
        compiler_params=_cparams(("arbitrary", "arbitrary")),
        name=name,
    )(*args)


def _qkv_kernel(a_ref, w_ref, g_ref, cos_ref, sa_ref, sb_ref, o_ref, ybuf, *, v_tiles, hd):
    i, j = pl.program_id(0), pl.program_id(1)

    @pl.when(jnp.logical_and(i == 0, j == 0))
    def _():
        ybuf[...] = jnp.zeros(ybuf.shape, ybuf.dtype)

    is_v = functools.reduce(jnp.logical_or, [j - 1 == t for t in v_tiles])

    def step(new, old):
        y = ybuf[old]
        ybuf[new] = jnp.dot(a_ref[...], w_ref[...].astype(BF16), preferred_element_type=F32)
        cos, sa, sb = cos_ref[...], sa_ref[...], sb_ref[...]
        for h in range(y.shape[1] // hd):
            yh = y[:, h * hd:(h + 1) * hd]
            yn = yh * lax.rsqrt(jnp.mean(yh * yh, axis=-1, keepdims=True) + EPS) * g_ref[:, h * hd:(h + 1) * hd]
            out = yn * cos + pltpu.roll(yn, hd - 1, 1) * sa + pltpu.roll(yn, 1, 1) * sb
            o_ref[:, h * hd:(h + 1) * hd] = jnp.where(is_v, yh, out).astype(o_ref.dtype)

    @pl.when(j % 2 == 0)
    def _():
        step(0, 1)

    @pl.when(j % 2 == 1)
    def _():
        step(1, 0)


def _qkv(h, w_in, gains, cos, sa, sb, v_tiles, tm, tn):
    m, d = h.shape
    n = w_in.shape[1]
    nt = n // tn
    tab = pl.BlockSpec((tm, HEAD_DIM), lambda i, j: (i, 0))
    prev = lambda i, j: (0, jnp.maximum(j - 1, 0))
    return pl.pallas_call(
        functools.partial(_qkv_kernel, v_tiles=v_tiles, hd=HEAD_DIM),
        grid=(m // tm, nt + 1),
        in_specs=[pl.BlockSpec((tm, d), lambda i, j: (i, 0)),
                  pl.BlockSpec((d, tn), lambda i, j: (0, jnp.minimum(j, nt - 1))),
                  pl.BlockSpec((1, tn), prev),
                  tab, tab, tab],
        out_specs=pl.BlockSpec((tm, tn), lambda i, j: (i, jnp.maximum(j - 1, 0))),
        out_shape=jax.ShapeDtypeStruct((m, n), BF16),
        scratch_shapes=[pltpu.VMEM((2, tm, tn), F32)],
        compiler_params=_cparams(("arbitrary", "arbitrary")),
        name="qkv_proj",
    )(h, w_in, gains, cos, sa, sb)


def _gateup_kernel(a_ref, wg_ref, wu_ref, o_ref):
    a = a_ref[...]
    g = jnp.dot(a, wg_ref[...].astype(BF16), preferred_element_type=F32)
    u = jnp.dot(a, wu_ref[...].astype(BF16), preferred_element_type=F32)
    o_ref[...] = (g * jax.nn.sigmoid(g) * u).astype(o_ref.dtype)


def _gateup(h, wg, wu, tm, tn):
    m, d = h.shape
    n = wg.shape[1]
    wspec = pl.BlockSpec((d, tn), lambda i, j: (0, j))
    return pl.pallas_call(
        _gateup_kernel,
        grid=(m // tm, n // tn),
        in_specs=[pl.BlockSpec((tm, d), lambda i, j: (i, 0)), wspec, wspec],
        out_specs=pl.BlockSpec((tm, tn), lambda i, j: (i, j)),
        out_shape=jax.ShapeDtypeStruct((m, n), BF16),
        compiler_params=_cparams(("arbitrary", "arbitrary")),
        name="ffn_gateup",
    )(h, wg, wu)


def _online_softmax_step(s, m_sc, rows):
    m_old = m_sc[rows, :]
    m_new = jnp.maximum(m_old, jnp.max(s, axis=-1, keepdims=True))
    m_sc[rows, :] = m_new
    return jnp.exp2(s - m_new), jnp.exp2(m_old - m_new)


CHAIN_HEADS = 1


def _chunk_loop(body, n_chunks):
    lax.fori_loop(0, n_chunks, body, 0, unroll=2 if n_chunks % 2 == 0 else 1)


def _gqa_kernel(q_ref, k_ref, v_ref, o_ref, vaug, m_sc, acc_sc, *, tq, tk, n_chunks, n_rep, hd):
    @pl.when(pl.program_id(1) == 0)
    def _():
        vaug[:, :hd] = v_ref[...]
        vaug[:, hd:] = jnp.ones((vaug.shape[0], hd), vaug.dtype)

    q = jnp.concatenate([q_ref[:, h * hd:(h + 1) * hd] for h in range(n_rep)], axis=0)
    m_sc[...] = jnp.full(m_sc.shape, -jnp.inf, F32)
    acc_sc[...] = jnp.zeros(acc_sc.shape, F32)

    def body(c, carry):
        start = pl.multiple_of(c * tk, tk)
        k = k_ref[pl.ds(start, tk), :]
        v = vaug[pl.ds(start, tk), :]
        cr = min(CHAIN_HEADS, n_rep) * tq
        for r0 in range(0, n_rep * tq, cr):
            rows = pl.ds(r0, cr)
            s = lax.dot_general(q[r0:r0 + cr], k, NT_DIMS, preferred_element_type=F32)
            p, alpha = _online_softmax_step(s, m_sc, rows)
            acc_sc[rows, :] = alpha * acc_sc[rows, :] + jnp.dot(p.astype(BF16), v, preferred_element_type=F32)
        return carry

    _chunk_loop(body, n_chunks)
    acc = acc_sc[...]
    o = acc[:, :hd] / acc[:, hd:]
    for h in range(n_rep):
        o_ref[:, h * hd:(h + 1) * hd] = o[h * tq:(h + 1) * tq].astype(o_ref.dtype)


def _gqa_attention(qkv, *, n_q, q_blk0, n_k, k_blk0, tq, tk, n_kv, n_rep, q_col0, k_col0, v_col0):
    hd = HEAD_DIM
    qw = n_rep * hd
    return pl.pallas_call(
        functools.partial(_gqa_kernel, tq=tq, tk=tk, n_chunks=n_k // tk, n_rep=n_rep, hd=hd),
        grid=(n_kv, n_q // tq),
        in_specs=[pl.BlockSpec((tq, qw), lambda g, i: (q_blk0 + i, q_col0 // qw + g)),
                  pl.BlockSpec((n_k, hd), lambda g, i: (k_blk0, k_col0 // hd + g)),
                  pl.BlockSpec((n_k, hd), lambda g, i: (k_blk0, v_col0 // hd + g))],
        out_specs=pl.BlockSpec((tq, qw), lambda g, i: (i, g)),
        out_shape=jax.ShapeDtypeStruct((n_q, n_kv * qw), BF16),
        scratch_shapes=[pltpu.VMEM((n_k, 2 * hd), BF16),
                        pltpu.VMEM((n_rep * tq, 1), F32),
                        pltpu.VMEM((n_rep * tq, 2 * hd), F32)],
        compiler_params=_cparams(("arbitrary", "arbitrary")),
        name="gqa_attention",
    )(qkv, qkv, qkv)


def _diff_kernel(q_ref, k_ref, v_ref, lam_ref, sub_ref, o_ref, m_sc, l_sc, acc_sc, *, tq, tk, n_chunks, hd,
                 lam_init):
    q1, q2 = q_ref[:, :hd], q_ref[:, hd:]
    m_sc[...] = jnp.full(m_sc.shape, -jnp.inf, F32)
    l_sc[...] = jnp.zeros(l_sc.shape, F32)
    acc_sc[...] = jnp.zeros(acc_sc.shape, F32)

    def body(c, carry):
        start = pl.multiple_of(c * tk, tk)
        k = k_ref[pl.ds(start, tk), :]
        v = v_ref[pl.ds(start, tk), :]
        s = jnp.concatenate([lax.dot_general(q1, k[:, :hd], NT_DIMS, preferred_element_type=F32),
                             lax.dot_general(q2, k[:, hd:], NT_DIMS, preferred_element_type=F32)], axis=0)
        rows = pl.ds(0, 2 * tq)
        p, alpha = _online_softmax_step(s, m_sc, rows)
        l_sc[...] = alpha * l_sc[...] + jnp.sum(p, axis=-1, keepdims=True)
        acc_sc[...] = alpha * acc_sc[...] + jnp.dot(p.astype(BF16), v, preferred_element_type=F32)
        return carry

    _chunk_loop(body, n_chunks)
    o = acc_sc[...] / l_sc[...]
    lp = lam_ref[...]
    lam = (jnp.exp(jnp.sum(lp[0:1] * lp[1:2], axis=-1, keepdims=True))
           - jnp.exp(jnp.sum(lp[2:3] * lp[3:4], axis=-1, keepdims=True)) + lam_init)
    o = o[:tq] - lam * o[tq:]
    o = o * lax.rsqrt(jnp.mean(o * o, axis=-1, keepdims=True) + EPS) * sub_ref[...] * (1.0 - lam_init)
    o_ref[...] = o.astype(o_ref.dtype)


def _diff_attention(qkv, lam_params, subln, *, n_q, q_blk0, n_k, k_blk0, tq, tk, n_heads,
                    q_col0, k_col0, v_col0, lam_init):
    hd = HEAD_DIM
    w2 = 2 * hd
    return pl.pallas_call(
        functools.partial(_diff_kernel, tq=tq, tk=tk, n_chunks=n_k // tk, hd=hd, lam_init=lam_init),
        grid=(n_heads, n_q // tq),
        in_specs=[pl.BlockSpec((tq, w2), lambda h, i: (q_blk0 + i, q_col0 // w2 + h)),
                  pl.BlockSpec((n_k, w2), lambda h, i: (k_blk0, k_col0 // w2 + h)),
                  pl.BlockSpec((n_k, w2), lambda h, i: (k_blk0, v_col0 // w2 + h)),
                  pl.BlockSpec((8, hd), lambda h, i: (0, 0)),
                  pl.BlockSpec((1, w2), lambda h, i: (0, 0))],
        out_specs=pl.BlockSpec((tq, w2), lambda h, i: (i, h)),
        out_shape=jax.ShapeDtypeStruct((n_q, n_heads * w2), BF16),
        scratch_shapes=[pltpu.VMEM((2 * tq, 1), F32), pltpu.VMEM((2 * tq, 1), F32),
                        pltpu.VMEM((2 * tq, w2), F32)],
        compiler_params=_cparams(("arbitrary", "arbitrary")),
        name="diff_attention",
    )(qkv, qkv, qkv, lam_params, subln)


def _compose_scan(a, b, reverse, group):
    n = a.shape[0]
    pos = lax.broadcasted_iota(jnp.int32, (n, 1), 0) & (group - 1)
    d = 1
    while d < group:
        if reverse:
            a_sh, b_sh, keep = pltpu.roll(a, n - d, 0), pltpu.roll(b, n - d, 0), pos < group - d
        else:
            a_sh, b_sh, keep = pltpu.roll(a, d, 0), pltpu.roll(b, d, 0), pos >= d
        b = jnp.where(keep, a * b_sh + b, b)
        a = jnp.where(keep, a * a_sh, a)
        d *= 2
    return a, b


def _compose_scan_sublanes(a, b, reverse):
    pos = lax.broadcasted_iota(jnp.int32, (1, SUBLANES, 1), 1)
    d = 1
    while d < SUBLANES:
        if reverse:
            a_sh, b_sh, keep = pltpu.roll(a, SUBLANES - d, 1), pltpu.roll(b, SUBLANES - d, 1), pos < SUBLANES - d
        else:
            a_sh, b_sh, keep = pltpu.roll(a, d, 1), pltpu.roll(b, d, 1), pos >= d
        b = jnp.where(keep, a * b_sh + b, b)
        a = jnp.where(keep, a * a_sh, a)
        d *= 2
    return a, b


def _scan_chunk(a, b, h0, reverse, a_sc, b_sc):
    n, c = a.shape
    ng = n // SUBLANES
    a, b = _compose_scan_sublanes(a.reshape(ng, SUBLANES, c), b.reshape(ng, SUBLANES, c), reverse)
    a, b = a.reshape(n, c), b.reshape(n, c)
    edge = 0 if reverse else SUBLANES - 1
    ag, bg = [], []
    for lb in range(c // LANES):
        a_sc[lb, 0:n, :] = a[:, lb * LANES:(lb + 1) * LANES]
        b_sc[lb, 0:n, :] = b[:, lb * LANES:(lb + 1) * LANES]
        ag.append(a_sc[lb, pl.ds(edge, ng, stride=SUBLANES), :])
        bg.append(b_sc[lb, pl.ds(edge, ng, stride=SUBLANES), :])
    ag, bg = jnp.concatenate(ag, axis=1), jnp.concatenate(bg, axis=1)
    ag, bg = _compose_scan(ag, bg, reverse, ng)
    hg = ag * h0 + bg
    grow = lax.broadcasted_iota(jnp.int32, (ng, 1), 0)
    if reverse:
        h_in = jnp.where(grow == ng - 1, h0, pltpu.roll(hg, ng - 1, 0))
        carry = hg[0:1]
    else:
        h_in = jnp.where(grow == 0, h0, pltpu.roll(hg, 1, 0))
        carry = hg[ng - 1:ng]
    h_in = jnp.broadcast_to(h_in[:, None, :], (ng, SUBLANES, c)).reshape(n, c)
    return a * h_in + b, carry


def _rglru_kernel(x_ref, gy_ref, cw_ref, cb_ref, gaw_ref, gab_ref, gxw_ref, gxb_ref, lam_ref, o_ref,
                  s_ref, a_sc, b_sc, *, n_lat, n_ctx, tc):
    n_rows = n_lat + n_ctx
    cw, cb = cw_ref[...], cb_ref[...]
    z = -lam_ref[...]
    softplus = jnp.maximum(z, 0.0) + jnp.log(1.0 + jnp.exp(-jnp.abs(z)))

    def conv(t0, n, seg0, seg1):
        xs = x_ref[pl.ds(t0, n), :]
        prev = x_ref[pl.ds(pl.multiple_of(jnp.maximum(t0 - 8, 0), 8), 8), :]
        nxt = x_ref[pl.ds(pl.multiple_of(jnp.minimum(t0 + n, n_rows - 8), 8), 8), :]
        prev = jnp.where(t0 > seg0, prev, 0.0)
        nxt = jnp.where(t0 + n < seg1, nxt, 0.0)
        ext = jnp.concatenate([prev, xs, nxt], axis=0)
        return (ext[6:6 + n] * cw[0:1] + ext[7:7 + n] * cw[1:2] + ext[8:8 + n] * cw[2:3]
                + ext[9:9 + n] * cw[3:4] + cb)

    for d in range(2):
        reverse = d == 1
        wa = gaw_ref[d, 0].astype(BF16)
        wx = gxw_ref[d, 0].astype(BF16)
        ba, bx = gab_ref[d:d + 1, :], gxb_ref[d:d + 1, :]
        c8 = -RG_C * softplus[d:d + 1, :]

        def chunk(t0, n, seg0, seg1, h0, reverse=reverse, wa=wa, wx=wx, ba=ba, bx=bx, c8=c8):
            xc = conv(t0, n, seg0, seg1)
            xb = xc.astype(BF16)
            r = jax.nn.sigmoid(jnp.dot(xb, wa, preferred_element_type=F32) + ba)
            i = jax.nn.sigmoid(jnp.dot(xb, wx, preferred_element_type=F32) + bx)
            a = jnp.exp(c8 * r)
            b = jnp.sqrt(1.0 - a * a) * (i * xc)
            return _scan_chunk(a, b, h0, reverse, a_sc, b_sc)

        _, h_end = chunk(n_lat, n_ctx, n_lat, n_rows, jnp.zeros((1, cw.shape[1]), F32))
        n_ch = n_lat // tc

        def body(ci, hc, chunk=chunk, reverse=reverse):
            cidx = (n_ch - 1 - ci) if reverse else ci
            t0 = pl.multiple_of(cidx * tc, tc)
            h, hc = chunk(t0, tc, 0, n_lat, hc)
            if reverse:
                s = s_ref[pl.ds(t0, tc), :] + h
                o_ref[pl.ds(t0, tc), :] = (s * gy_ref[pl.ds(t0, tc), :].astype(F32)).astype(o_ref.dtype)
            else:
                s_ref[pl.ds(t0, tc), :] = h
            return hc

        lax.fori_loop(0, n_ch, body, h_end)


def _rglru(xr, gy, conv_w, conv_b, ga_w, ga_b, gx_w, gx_b, lam, n_lat, n_ctx):
    n_rows, c_tot = xr.shape
    nb, bw = ga_w.shape[1], ga_w.shape[2]
    tc = _tile(n_lat, 512, 64)
    assert n_ctx % 64 == 0
    col = lambda j: (0, j)
    return pl.pallas_call(
        functools.partial(_rglru_kernel, n_lat=n_lat, n_ctx=n_ctx, tc=tc),
        grid=(nb,),
        in_specs=[pl.BlockSpec((n_rows, bw), col),
                  pl.BlockSpec((n_lat, bw), col),
                  pl.BlockSpec((conv_w.shape[0], bw), col),
                  pl.BlockSpec((1, bw), col),
                  pl.BlockSpec((2, 1, bw, bw), lambda j: (0, j, 0, 0)),
                  pl.BlockSpec((2, bw), col),
                  pl.BlockSpec((2, 1, bw, bw), lambda j: (0, j, 0, 0)),
                  pl.BlockSpec((2, bw), col),
                  pl.BlockSpec((2, bw), col)],
        out_specs=pl.BlockSpec((n_lat, bw), col),
        out_shape=jax.ShapeDtypeStruct((n_lat, c_tot), BF16),
        scratch_shapes=[pltpu.VMEM((n_lat, bw), F32),
                        pltpu.VMEM((bw // LANES, max(tc, n_ctx), LANES), F32),
                        pltpu.VMEM((bw // LANES, max(tc, n_ctx), LANES), F32)],
        compiler_params=_cparams(("arbitrary",)),
        name="rglru",
    )(xr, gy, conv_w, conv_b.reshape(1, c_tot), ga_w, ga_b, gx_w, gx_b, lam)


def _row_copy(src_hbm, row, dst, r, sem):
    return pltpu.make_async_copy(src_hbm.at[pl.ds(row, 1)], dst.at[pl.ds(r, 1)], sem)


def _gather_kernel(src_ref, nu_ref, h_hbm, o_ref, buf, sem, *, tm):
    t = pl.program_id(0)
    n_used = nu_ref[0]

    def issue(tile, slot):
        def start(r, c):
            _row_copy(h_hbm, src_ref[tile * tm + r], buf.at[slot], r, sem.at[slot]).start()
            return c
        lax.fori_loop(0, tm, start, 0)

    @pl.when(t == 0)
    def _():
        issue(0, 0)

    @pl.when(t + 1 < n_used)
    def _():
        issue(t + 1, (t + 1) % 2)

    @pl.when(t < n_used)
    def _():
        slot = t % 2

        def wait(r, c):
            _row_copy(h_hbm, 0, buf.at[slot], r, sem.at[slot]).wait()
            return c
        lax.fori_loop(0, tm, wait, 0)
        o_ref[...] = buf[slot].astype(o_ref.dtype)

    @pl.when(t >= n_used)
    def _():
        o_ref[...] = jnp.zeros(o_ref.shape, o_ref.dtype)


def _moe_gather(h, src_tok, n_used, n_rows, tm):
    d = h.shape[1]
    return pl.pallas_call(
        functools.partial(_gather_kernel, tm=tm),
        grid_spec=pltpu.PrefetchScalarGridSpec(
            num_scalar_prefetch=2,
            grid=(n_rows // tm,),
            in_specs=[pl.BlockSpec(memory_space=pl.ANY)],
            out_specs=pl.BlockSpec((tm, d), lambda t, src, nu: (t, 0)),
            scratch_shapes=[pltpu.VMEM((2, tm, d), F32), pltpu.SemaphoreType.DMA((2,))]),
        out_shape=jax.ShapeDtypeStruct((n_rows, d), BF16),
        compiler_params=_cparams(("arbitrary",)),
        name="moe_gather",
    )(src_tok, n_used, h)


def _expert_rows_kernel(ts_ref, nt_ref, nu_ref, x_hbm, *refs, n_w, tm, tn, compute):
    w_refs, o_hbm = refs[:n_w], refs[n_w]
    wb = refs[n_w + 1:2 * n_w + 1]
    xbuf, obuf, sem_in, sem_out = refs[2 * n_w + 1:]
    j, e = pl.program_id(0), pl.program_id(1)
    nt = nt_ref[e]
    row0 = ts_ref[e] * tm
    col0 = pl.multiple_of(j * tn, tn)

    tu = 2 * tm
    n_full = nt // 2
    tail = nt - 2 * n_full

    def in_copy(u, rows, slot):
        r = pl.multiple_of(row0 + u * tu, tm)
        return pltpu.make_async_copy(x_hbm.at[pl.ds(r, rows)], xbuf.at[slot, pl.ds(0, rows)], sem_in.at[slot])

    def out_copy(u, rows, slot):
        r = pl.multiple_of(row0 + u * tu, tm)
        return pltpu.make_async_copy(obuf.at[slot, pl.ds(0, rows)], o_hbm.at[pl.ds(r, rows), pl.ds(col0, tn)],
                                     sem_out.at[slot])

    def run(rows, slot):
        x = xbuf[slot, pl.ds(0, rows), :]
        obuf[slot, pl.ds(0, rows), :] = compute(x, *[w_b[...] for w_b in wb]).astype(obuf.dtype)

    @pl.when(nt > 0)
    def _():
        for w_ref, w_b in zip(w_refs, wb):
            w_b[...] = w_ref[0].astype(BF16)

    @pl.when(n_full > 0)
    def _():
        in_copy(0, tu, 0).start()

        def body(u, c):
            slot = u % 2

            @pl.when(u + 1 < n_full)
            def _():
                in_copy(u + 1, tu, 1 - slot).start()

            in_copy(u, tu, slot).wait()

            @pl.when(u >= 2)
            def _():
                out_copy(u - 2, tu, slot).wait()

            run(tu, slot)
            out_copy(u, tu, slot).start()
            return c

        lax.fori_loop(0, n_full, body, 0)

        @pl.when(n_full >= 2)
        def _():
            out_copy(n_full - 2, tu, n_full % 2).wait()
        out_copy(n_full - 1, tu, (n_full - 1) % 2).wait()

    @pl.when(tail > 0)
    def _():
        cp_in = in_copy(n_full, tm, 0)
        cp_in.start()
        cp_in.wait()
        run(tm, 0)
        cp_out = out_copy(n_full, tm, 0)
        cp_out.start()
        cp_out.wait()

    @pl.when(e == pl.num_programs(1) - 1)
    def _():
        obuf[0] = jnp.zeros(obuf.shape[1:], obuf.dtype)

        def zero_tile(t, c):
            cp = pltpu.make_async_copy(obuf.at[0, pl.ds(0, tm)],
                                       o_hbm.at[pl.ds(pl.multiple_of(t * tm, tm), tm), pl.ds(col0, tn)],
                                       sem_out.at[0])
            cp.start()
            cp.wait()
            return c

        lax.fori_loop(nu_ref[0], o_hbm.shape[0] // tm, zero_tile, 0)


def _swiglu_tile(x, wg, wu):
    g = jnp.dot(x, wg, preferred_element_type=F32)
    u = jnp.dot(x, wu, preferred_element_type=F32)
    return g * jax.nn.sigmoid(g) * u


def _down_tile(x, wd):
    return jnp.dot(x, wd, preferred_element_type=F32)


def _moe_expert_matmul(tile_start, n_tiles, n_used, xs, weights, compute, out_dtype, tm, tn, name):
    n_rows, k = xs.shape
    n_exp, _, n_out = weights[0].shape
    n_w = len(weights)
    wspec = pl.BlockSpec((1, k, tn), lambda j, e, ts, nt, nu: (e, 0, j))
    any_spec = pl.BlockSpec(memory_space=pl.ANY)
    return pl.pallas_call(
        functools.partial(_expert_rows_kernel, n_w=n_w, tm=tm, tn=tn, compute=compute),
        grid_spec=pltpu.PrefetchScalarGridSpec(
            num_scalar_prefetch=3,
            grid=(n_out // tn, n_exp),
            in_specs=[any_spec] + [wspec] * n_w,
            out_specs=any_spec,
            scratch_shapes=[pltpu.VMEM((k, tn), BF16)] * n_w + [
                pltpu.VMEM((2, 2 * tm, k), xs.dtype), pltpu.VMEM((2, 2 * tm, tn), out_dtype),
                pltpu.SemaphoreType.DMA((2,)), pltpu.SemaphoreType.DMA((2,))]),
        out_shape=jax.ShapeDtypeStruct((n_rows, n_out), out_dtype),
        compiler_params=_cparams(("arbitrary", "arbitrary")),
        name=name,
    )(tile_start, n_tiles, n_used, xs, *weights)


def _combine_kernel(dest_ref, x_ref, g_ref, w_ref, y_hbm, o_ref, buf, sem, *, tt):
    i = pl.program_id(0)

    def issue(tile, slot):
        def start(r, c):
            tok = tile * tt + r
            for kk in range(TOP_K):
                _row_copy(y_hbm, dest_ref[TOP_K * tok + kk], buf.at[slot, kk], r, sem.at[slot]).start()
            return c
        lax.fori_loop(0, tt, start, 0)

    @pl.when(i == 0)
    def _():
        issue(0, 0)

    @pl.when(i + 1 < pl.num_programs(0))
    def _():
        issue(i + 1, (i + 1) % 2)

    slot = i % 2

    def wait(r, c):
        for kk in range(TOP_K):
            _row_copy(y_hbm, 0, buf.at[slot, kk], r, sem.at[slot]).wait()
        return c
    lax.fori_loop(0, tt, wait, 0)
    w = w_ref[...]
    mix = w[:, 0:1] * buf[slot, 0] + w[:, 1:2] * buf[slot, 1]
    o_ref[...] = x_ref[...] + g_ref[0:1, :] * mix


def _moe_combine(dest, x, gate, w_top, y, n_tok, tt):
    d = x.shape[1]
    return pl.pallas_call(
        functools.partial(_combine_kernel, tt=tt),
        grid_spec=pltpu.PrefetchScalarGridSpec(
            num_scalar_prefetch=1,
            grid=(n_tok // tt,),
            in_specs=[pl.BlockSpec((tt, d), lambda i, dst: (i, 0)),
                      pl.BlockSpec((2, d), lambda i, dst: (0, 0)),
                      pl.BlockSpec((tt, LANES), lambda i, dst: (i, 0)),
                      pl.BlockSpec(memory_space=pl.ANY)],
            out_specs=pl.BlockSpec((tt, d), lambda i, dst: (i, 0)),
            scratch_shapes=[pltpu.VMEM((2, TOP_K, tt, d), F32), pltpu.SemaphoreType.DMA((2,))]),
        out_shape=jax.ShapeDtypeStruct((n_tok, d), F32),
        compiler_params=_cparams(("arbitrary",)),
        name="moe_combine",
    )(dest, x, gate, w_top, y)


def _moe_plan(e_idx, n_exp, tm):
    n_tok = e_idx.shape[0]
    n_slot = n_tok * TOP_K
    flat_e = e_idx.reshape(n_slot)
    onehot = (flat_e[:, None] == jnp.arange(n_exp, dtype=jnp.int32)[None, :]).astype(jnp.int32)
    csum = jnp.cumsum(onehot, axis=0)
    rank = jnp.sum(csum * onehot, axis=1) - 1
    counts = csum[-1]
    n_tiles = (counts + tm - 1) // tm
    tile_end = jnp.cumsum(n_tiles)
    tile_start = tile_end - n_tiles
    dest = (tile_start[flat_e] * tm + rank).astype(jnp.int32)
    n_rows = (n_slot // tm + n_exp) * tm
    src_tok = jnp.zeros((n_rows,), jnp.int32).at[dest].set(jnp.arange(n_slot, dtype=jnp.int32) // TOP_K)
    return dest, src_tok, tile_start.astype(jnp.int32), n_tiles.astype(jnp.int32), tile_end[-1:].astype(jnp.int32), n_rows


def _rope_tables(n_lat, n_ctx):
    half = HEAD_DIM // 2
    inv = ROPE_THETA ** (-jnp.arange(0, half, 2, dtype=F32) / half)
    rows = n_lat // GRID_W
    row = jnp.repeat(jnp.arange(rows, dtype=F32), GRID_W)
    col = jnp.tile(jnp.arange(GRID_W, dtype=F32), rows)
    ang = jnp.concatenate([row[:, None] * inv, col[:, None] * inv], axis=-1)
    cos = jnp.repeat(jnp.cos(ang), 2, axis=-1)
    sin = jnp.repeat(jnp.sin(ang), 2, axis=-1)
    even = (jnp.arange(HEAD_DIM) % 2 == 0)[None, :]
    sa = jnp.where(even, -sin, 0.0)
    sb = jnp.where(even, 0.0, sin)
    ident = jnp.ones((n_ctx, HEAD_DIM), F32)
    zero = jnp.zeros((n_ctx, HEAD_DIM), F32)
    return (jnp.concatenate([cos, ident]), jnp.concatenate([sa, zero]), jnp.concatenate([sb, zero]))


def kernel(x, c, ctx, c_ctx, attn_w_mod, attn_b_mod, attn_norm_mix, attn_norm_ffn, attn_w_in, attn_gqa_q_norm, attn_gqa_k_norm, attn_diff_q_norm, attn_diff_k_norm, attn_diff_lambda_q1, attn_diff_lambda_k1, attn_diff_lambda_q2, attn_diff_lambda_k2, attn_diff_subln, attn_w_out, ffn_w_gate, ffn_w_up, ffn_w_down, rec_w_mod, rec_b_mod, rec_norm_mix, rec_norm_ffn, rec_w_in, rec_conv_w, rec_conv_b, rec_gate_a_w, rec_gate_a_b, rec_gate_x_w, rec_gate_x_b, rec_lru_lambda, rec_w_out, moe_router_w, moe_router_b, moe_w_gate, moe_w_up, moe_w_down):
    assert x.shape[0] == 1 and attn_w_in.shape[0] == 1 and rec_w_in.shape[0] == 1
    n_lat, d = x.shape[1], x.shape[2]
    n_ctx = ctx.shape[1]
    n_rows = n_lat + n_ctx
    hd = HEAD_DIM
    gqa_heads, diff_heads = d // 256, d // 512
    kv_heads = gqa_heads // 4
    n_rep = gqa_heads // kv_heads
    gq_w, gkv_w = gqa_heads * hd, kv_heads * hd
    dq_w, dv_w = diff_heads * 2 * hd, diff_heads * 2 * hd
    col_ka, col_va, col_qb = gq_w, gq_w + gkv_w, gq_w + 2 * gkv_w
    col_kb, col_vb = col_qb + dq_w, col_qb + 2 * dq_w
    d_ff = ffn_w_gate.shape[2]
    n_exp = moe_router_w.shape[2]

    xs = jnp.concatenate([x[0], ctx[0]], axis=0)
    c2 = jnp.zeros((8, d), F32).at[0].set(c[0]).at[1].set(c_ctx)
    tm_big = _tile(n_rows, 1056, 16)

    mods = _modvec(c2, attn_w_mod[0], attn_b_mod[0])[:2].reshape(2, 6, d)
    sh1, sc1, g1, sh2, sc2, g2 = (mods[:, j] for j in range(6))
    h = _normmod(xs, attn_norm_mix[0], sh1, sc1, n_lat)

    scale = hd ** -0.5 * math.log2(math.e)
    tn_qkv = min(4 * hd, gkv_w)
    gains = jnp.concatenate([
        jnp.tile(attn_gqa_q_norm[0] * scale, gqa_heads), jnp.tile(attn_gqa_k_norm[0], kv_heads),
        jnp.ones((gkv_w,), F32),
        jnp.tile((attn_diff_q_norm[0] * scale).reshape(-1), diff_heads),
        jnp.tile(attn_diff_k_norm[0].reshape(-1), diff_heads), jnp.ones((dv_w,), F32)]).reshape(1, -1)
    v_tiles = tuple(range(col_va // tn_qkv, col_qb // tn_qkv)) + tuple(
        range(col_vb // tn_qkv, (col_vb + dv_w) // tn_qkv))
    cos, sa, sb = _rope_tables(n_lat, n_ctx)
    qkv = _qkv(h, attn_w_in[0], gains, cos, sa, sb, v_tiles, tm_big, tn_qkv)

    tq_ctx = n_ctx
    tq_gqa = _tile(n_lat, 512, 16)
    tq_diff = _tile(n_lat, 512, 16)
    tk_lat = _tile(n_rows, 1536, LANES)
    layer = 0
    lam_init = 0.8 - 0.6 * math.exp(-0.3 * layer)
    lam_params = jnp.zeros((8, hd), F32).at[0].set(attn_diff_lambda_q1[0]).at[1].set(attn_diff_lambda_k1[0]) \
        .at[2].set(attn_diff_lambda_q2[0]).at[3].set(attn_diff_lambda_k2[0])
    subln = attn_diff_subln[0].reshape(1, 2 * hd)
    gqa_args = dict(n_kv=kv_heads, n_rep=n_rep, q_col0=0, k_col0=col_ka, v_col0=col_va)
    diff_args = dict(n_heads=diff_heads, q_col0=col_qb, k_col0=col_kb, v_col0=col_vb, lam_init=lam_init)
    lat_args = dict(n_q=n_lat, q_blk0=0, n_k=n_rows, k_blk0=0, tk=tk_lat)
    ctx_args = dict(n_q=n_ctx, q_blk0=n_lat // tq_ctx, n_k=n_ctx, k_blk0=n_lat // n_ctx, tk=n_ctx, tq=tq_ctx)
    oa = jnp.concatenate([_gqa_attention(qkv, tq=tq_gqa, **lat_args, **gqa_args),
                          _gqa_attention(qkv, **ctx_args, **gqa_args)])
    ob = jnp.concatenate([_diff_attention(qkv, lam_params, subln, tq=tq_diff, **lat_args, **diff_args),
                          _diff_attention(qkv, lam_params, subln, **ctx_args, **diff_args)])

    x1 = _mm([(oa, gq_w), (ob, dv_w)], attn_w_out[0], m=n_rows, tm=tm_big, tn=512, n_out=d, out_dtype=F32,
             resid=xs, gate=g1, n_lat=n_lat, name="attn_out")
    h = _normmod(x1, attn_norm_ffn[0], sh2, sc2, n_lat)
    hff = _gateup(h, ffn_w_gate[0], ffn_w_up[0], tm_big, 256)
    k_half = d_ff // 2
    x2 = x1
    for kb in range(2):
        x2 = _mm([(hff, k_half)], ffn_w_down[0], m=n_rows, tm=tm_big, tn=256, n_out=d, out_dtype=F32,
                 w_row_blk=kb, a_col_blk=[kb], resid=x2, gate=g2, n_lat=n_lat, name="ffn_down")

    mods = _modvec(c2, rec_w_mod[0], rec_b_mod[0])[:2].reshape(2, 6, d)
    sh1, sc1, g1, sh2, sc2, g2 = (mods[:, j] for j in range(6))
    h = _normmod(x2, rec_norm_mix[0], sh1, sc1, n_lat)
    d_rnn = rec_w_out.shape[1]
    gy = _mm([(h, d)], rec_w_in[0], m=n_rows, tm=tm_big, tn=512, n_out=d_rnn, out_dtype=BF16,
             act="gelu", name="rec_in_y")
    xr = _mm([(h, d)], rec_w_in[0], m=n_rows, tm=tm_big, tn=512, n_out=d_rnn, out_dtype=F32,
             w_col0=d_rnn, name="rec_in_x")
    s = _rglru(xr, gy, rec_conv_w[0], rec_conv_b[0], rec_gate_a_w[0], rec_gate_a_b[0],
               rec_gate_x_w[0], rec_gate_x_b[0], rec_lru_lambda[0], n_lat, n_ctx)
    tm_lat = _tile(n_lat, 1024, 16)
    x3 = _mm([(s, d_rnn)], rec_w_out[0], m=n_lat, tm=tm_lat, tn=512, n_out=d, out_dtype=F32,
             resid=x2, gate=g1, n_lat=n_lat, name="rec_out")

    h4, w_top, e_top = _normmod_router(x3, rec_norm_ffn[0], sh2, sc2, moe_router_w[0], moe_router_b[0], n_lat)
    tm_e = 256
    dest, src_tok, tile_start, n_tiles, n_used, n_sorted = _moe_plan(e_top[:, :TOP_K], n_exp, tm_e)
    xg = _moe_gather(h4, src_tok, n_used, n_sorted, tm_e)
    hs = _moe_expert_matmul(tile_start, n_tiles, n_used, xg, [moe_w_gate[0], moe_w_up[0]], _swiglu_tile, BF16,
                            tm_e, 512, "moe_gateup")
    ys = _moe_expert_matmul(tile_start, n_tiles, n_used, hs, [moe_w_down[0]], _down_tile, F32, tm_e,
                            _tile(d, 1024, LANES), "moe_down")
    out = _moe_combine(dest, x3, g2, w_top, ys, n_lat, 256)
    return out.reshape(1, n_lat, d)
```

```python
import functools
import math

import jax
import jax.numpy as jnp
from jax import lax
from jax.experimental import pallas as pl
from jax.experimental.pallas import tpu as pltpu

F32 = jnp.float32
BF16 = jnp.bfloat16
EPS = 1e-6
ROPE_THETA = 10000.0
GRID_W = 64
HEAD_DIM = 128
RG_C = 8.0
TOP_K = 2
LANES = 128
SUBLANES = 8
VMEM_LIMIT = 56 * 1024 * 1024
NT_DIMS = (((1,), (1,)), ((), ()))


def _cparams(sem, vmem=VMEM_LIMIT):
    return pltpu.CompilerParams(dimension_semantics=sem, vmem_limit_bytes=vmem)


def _tile(n, target, mult):
    best = None
    for t in range(mult, min(n, target) + 1, mult):
        if n % t == 0:
            best = t
    assert best is not None, (n, target, mult)
    return best


def _row_select(row0, tm, n_lat, ref):
    row = row0 + lax.broadcasted_iota(jnp.int32, (tm, 1), 0)
    return jnp.where(row < n_lat, ref[0:1, :], ref[1:2, :])


def _modvec_kernel(c_ref, w_ref, b_ref, o_ref):
    c = c_ref[...]
    a = (c * jax.nn.sigmoid(c)).astype(BF16)
    o_ref[...] = jnp.dot(a, w_ref[...].astype(BF16), preferred_element_type=F32) + b_ref[...]


def _modvec(c2, w_mod, b_mod):
    d, n = w_mod.shape
    tn = _tile(n, 512, LANES)
    return pl.pallas_call(
        _modvec_kernel,
        grid=(n // tn,),
        in_specs=[pl.BlockSpec((8, d), lambda j: (0, 0)),
                  pl.BlockSpec((d, tn), lambda j: (0, j)),
                  pl.BlockSpec((1, tn), lambda j: (0, j))],
        out_specs=pl.BlockSpec((8, tn), lambda j: (0, j)),
        out_shape=jax.ShapeDtypeStruct((8, n), F32),
        compiler_params=_cparams(("arbitrary",)),
        name="modvec",
    )(c2, w_mod, b_mod.reshape(1, n))


def _normmod_kernel(x_ref, g_ref, sh_ref, sc_ref, o_ref, *, tm, n_lat):
    x = x_ref[...]
    y = x * lax.rsqrt(jnp.mean(x * x, axis=-1, keepdims=True) + EPS) * g_ref[...]
    row0 = pl.program_id(0) * tm
    sc = _row_select(row0, tm, n_lat, sc_ref)
    sh = _row_select(row0, tm, n_lat, sh_ref)
    o_ref[...] = (y * (1.0 + sc) + sh).astype(o_ref.dtype)


def _normmod(x, g, sh, sc, n_lat, m=None):
    m = x.shape[0] if m is None else m
    d = x.shape[1]
    tm = _tile(m, 528, 16)
    vec = pl.BlockSpec((1, d), lambda i: (0, 0))
    two = pl.BlockSpec((2, d), lambda i: (0, 0))
    return pl.pallas_call(
        functools.partial(_normmod_kernel, tm=tm, n_lat=n_lat),
        grid=(m // tm,),
        in_specs=[pl.BlockSpec((tm, d), lambda i: (i, 0)), vec, two, two],
        out_specs=pl.BlockSpec((tm, d), lambda i: (i, 0)),
        out_shape=jax.ShapeDtypeStruct((m, d), BF16),
        compiler_params=_cparams(("arbitrary",)),
        name="normmod",
    )(x, g.reshape(1, d), sh, sc)


def _normmod_router_kernel(x_ref, g_ref, sh_ref, sc_ref, rw_ref, rb_ref, h_ref, w_ref, e_ref, *, n_exp):
    x = x_ref[...]
    y = x * lax.rsqrt(jnp.mean(x * x, axis=-1, keepdims=True) + EPS) * g_ref[...]
    h = y * (1.0 + sc_ref[0:1, :]) + sh_ref[0:1, :]
    h_ref[...] = h
    logits = jnp.dot(h, rw_ref[...], preferred_element_type=F32,
                     precision=lax.Precision.HIGHEST) + rb_ref[...]
    lane = lax.broadcasted_iota(jnp.int32, logits.shape, 1)
    lanef = lane.astype(F32)
    neg = jnp.float32(-jnp.inf)
    lg = jnp.where(lane < n_exp, logits, neg)
    m1 = jnp.max(lg, axis=-1, keepdims=True)
    i1 = jnp.min(jnp.where(lg == m1, lanef, float(LANES)), axis=-1, keepdims=True)
    lg2 = jnp.where(lanef == i1, neg, lg)
    m2 = jnp.max(lg2, axis=-1, keepdims=True)
    i2 = jnp.min(jnp.where(lg2 == m2, lanef, float(LANES)), axis=-1, keepdims=True)
    e = jnp.exp(m2 - m1)
    w1 = 1.0 / (1.0 + e)
    w2 = e / (1.0 + e)
    w_ref[...] = jnp.where(lane == 0, w1, jnp.where(lane == 1, w2, 0.0))
    e_ref[...] = jnp.where(lane == 0, i1, jnp.where(lane == 1, i2, 0.0)).astype(jnp.int32)


def _normmod_router(x, g, sh, sc, router_w, router_b, m):
    d = x.shape[1]
    n_exp = router_w.shape[1]
    tm = _tile(m, 256, 8)
    rw = jnp.zeros((d, LANES), F32).at[:, :n_exp].set(router_w)
    rb = jnp.zeros((1, LANES), F32).at[0, :n_exp].set(router_b)
    vec = pl.BlockSpec((1, d), lambda i: (0, 0))
    two = pl.BlockSpec((2, d), lambda i: (0, 0))
    return pl.pallas_call(
        functools.partial(_normmod_router_kernel, n_exp=n_exp),
        grid=(m // tm,),
        in_specs=[pl.BlockSpec((tm, d), lambda i: (i, 0)), vec, two, two,
                  pl.BlockSpec((d, LANES), lambda i: (0, 0)),
                  pl.BlockSpec((1, LANES), lambda i: (0, 0))],
        out_specs=[pl.BlockSpec((tm, d), lambda i: (i, 0)),
                   pl.BlockSpec((tm, LANES), lambda i: (i, 0)),
                   pl.BlockSpec((tm, LANES), lambda i: (i, 0))],
        out_shape=[jax.ShapeDtypeStruct((m, d), F32),
                   jax.ShapeDtypeStruct((m, LANES), F32),
                   jax.ShapeDtypeStruct((m, LANES), jnp.int32)],
        compiler_params=_cparams(("arbitrary",)),
        name="normmod_router",
    )(x, g.reshape(1, d), sh, sc, rw, rb)


def _mm_kernel(*refs, ks, act, has_res, tm, n_lat):
    n_a = len(ks)
    a_refs, w_ref = refs[:n_a], refs[n_a]
    o_ref = refs[-1]
    w = w_ref[...].astype(BF16)
    acc, off = None, 0
    for a_ref, k in zip(a_refs, ks):
        part = jnp.dot(a_ref[...], w[off:off + k], preferred_element_type=F32)
        acc = part if acc is None else acc + part
        off += k
    if act == "gelu":
        acc = jax.nn.gelu(acc, approximate=True)
    if has_res:
        x_ref, gate_ref = refs[n_a + 1], refs[n_a + 2]
        gate = _row_select(pl.program_id(0) * tm, tm, n_lat, gate_ref)
        acc = x_ref[...] + gate * acc
    o_ref[...] = acc.astype(o_ref.dtype)


def _mm(a_list, w, *, m, tm, tn, n_out, out_dtype, w_row_blk=0, w_col0=0, a_col_blk=None,
        act=None, resid=None, gate=None, n_lat=0, name="mm"):
    ks = tuple(k for _, k in a_list)
    arrs = [a for a, _ in a_list]
    kt = sum(ks)
    a_col_blk = a_col_blk or [0] * len(arrs)
    assert m % tm == 0 and n_out % tn == 0 and w_col0 % tn == 0
    c0 = w_col0 // tn
    in_specs = [pl.BlockSpec((tm, k), functools.partial(lambda i, j, cb: (i, cb), cb=cb))
                for k, cb in zip(ks, a_col_blk)]
    in_specs.append(pl.BlockSpec((kt, tn), lambda i, j: (w_row_blk, c0 + j)))
    args = arrs + [w]
    if resid is not None:
        in_specs += [pl.BlockSpec((tm, tn), lambda i, j: (i, j)),
                     pl.BlockSpec((2, tn), lambda i, j: (0, j))]
        args += [resid, gate]
    return pl.pallas_call(
        functools.partial(_mm_kernel, ks=ks, act=act, has_res=resid is not None, tm=tm, n_lat=n_lat),
        grid=(m // tm, n_out // tn),
        in_specs=in_specs,
        out_specs=pl.BlockSpec((tm, tn), lambda i, j: (i, j)),
        out_shape=jax.ShapeDtypeStruct((m, n_out), out_dtype),
        compiler_params=_cparams(("arbitrary", "arbitrary")),
        name=name,
    )(*args)


def _qkv_kernel(a_ref, w_ref, g_ref, cos_ref, sa_ref, sb_ref, o_ref, ybuf, *, v_tiles, hd):
    i, j = pl.program_id(0), pl.program_id(1)

    @pl.when(jnp.logical_and(i == 0, j == 0))
    def _():
        ybuf[...] = jnp.zeros(ybuf.shape, ybuf.dtype)

    is_v = functools.reduce(jnp.logical_or, [j - 1 == t for t in v_tiles])

    def step(new, old):
        y = ybuf[old]
        ybuf[new] = jnp.dot(a_ref[...], w_ref[...].astype(BF16), preferred_element_type=F32)
        cos, sa, sb = cos_ref[...], sa_ref[...], sb_ref[...]
        for h in range(y.shape[1] // hd):
            yh = y[:, h * hd:(h + 1) * hd]
            yn = yh * lax.rsqrt(jnp.mean(yh * yh, axis=-1, keepdims=True) + EPS) * g_ref[:, h * hd:(h + 1) * hd]
            out = yn * cos + pltpu.roll(yn, hd - 1, 1) * sa + pltpu.roll(yn, 1, 1) * sb
            o_ref[:, h * hd:(h + 1) * hd] = jnp.where(is_v, yh, out).astype(o_ref.dtype)

    @pl.when(j % 2 == 0)
    def _():
        step(0, 1)

    @pl.when(j % 2 == 1)
    def _():
        step(1, 0)


def _qkv(h, w_in, gains, cos, sa, sb, v_tiles, tm, tn):
    m, d = h.shape
    n = w_in.shape[1]
    nt = n // tn
    tab = pl.BlockSpec((tm, HEAD_DIM), lambda i, j: (i, 0))
    prev = lambda i, j: (0, jnp.maximum(j - 1, 0))
    return pl.pallas_call(
        functools.partial(_qkv_kernel, v_tiles=v_tiles, hd=HEAD_DIM),
        grid=(m // tm, nt + 1),
        in_specs=[pl.BlockSpec((tm, d), lambda i, j: (i, 0)),
                  pl.BlockSpec((d, tn), lambda i, j: (0, jnp.minimum(j, nt - 1))),
                  pl.BlockSpec((1, tn), prev),
                  tab, tab, tab],
        out_specs=pl.BlockSpec((tm, tn), lambda i, j: (i, jnp.maximum(j - 1, 0))),
        out_shape=jax.ShapeDtypeStruct((m, n), BF16),
        scratch_shapes=[pltpu.VMEM((2, tm, tn), F32)],
        compiler_params=_cparams(("arbitrary", "arbitrary")),
        name="qkv_proj",
    )(h, w_in, gains, cos, sa, sb)


def _gateup_kernel(a_ref, wg_ref, wu_ref, o_ref):
    a = a_ref[...]
    g = jnp.dot(a, wg_ref[...].astype(BF16), preferred_element_type=F32)
    u = jnp.dot(a, wu_ref[...].astype(BF16), preferred_element_type=F32)
    o_ref[...] = (g * jax.nn.sigmoid(g) * u).astype(o_ref.dtype)


def _gateup(h, wg, wu, tm, tn):
    m, d = h.shape
    n = wg.shape[1]
    wspec = pl.BlockSpec((d, tn), lambda i, j: (0, j))
    return pl.pallas_call(
        _gateup_kernel,
        grid=(m // tm, n // tn),
        in_specs=[pl.BlockSpec((tm, d), lambda i, j: (i, 0)), wspec, wspec],
        out_specs=pl.BlockSpec((tm, tn), lambda i, j: (i, j)),
        out_shape=jax.ShapeDtypeStruct((m, n), BF16),
        compiler_params=_cparams(("arbitrary", "arbitrary")),
        name="ffn_gateup",
    )(h, wg, wu)


def _online_softmax_step(s, m_sc, rows):
    m_old = m_sc[rows, :]
    m_new = jnp.maximum(m_old, jnp.max(s, axis=-1, keepdims=True))
    m_sc[rows, :] = m_new
    return jnp.exp2(s - m_new), jnp.exp2(m_old - m_new)


CHAIN_HEADS = 1


def _chunk_loop(body, n_chunks):
    lax.fori_loop(0, n_chunks, body, 0, unroll=2 if n_chunks % 2 == 0 else 1)


def _gqa_kernel(q_ref, k_ref, v_ref, o_ref, vaug, m_sc, acc_sc, *, tq, tk, n_chunks, n_rep, hd):
    @pl.when(pl.program_id(1) == 0)
    def _():
        vaug[:, :hd] = v_ref[...]
        vaug[:, hd:] = jnp.ones((vaug.shape[0], hd), vaug.dtype)

    q = jnp.concatenate([q_ref[:, h * hd:(h + 1) * hd] for h in range(n_rep)], axis=0)
    m_sc[...] = jnp.full(m_sc.shape, -jnp.inf, F32)
    acc_sc[...] = jnp.zeros(acc_sc.shape, F32)

    def body(c, carry):
        start = pl.multiple_of(c * tk, tk)
        k = k_ref[pl.ds(start, tk), :]
        v = vaug[pl.ds(start, tk), :]
        cr = min(CHAIN_HEADS, n_rep) * tq
        for r0 in range(0, n_rep * tq, cr):
            rows = pl.ds(r0, cr)
            s = lax.dot_general(q[r0:r0 + cr], k, NT_DIMS, preferred_element_type=F32)
            p, alpha = _online_softmax_step(s, m_sc, rows)
            acc_sc[rows, :] = alpha * acc_sc[rows, :] + jnp.dot(p.astype(BF16), v, preferred_element_type=F32)
        return carry

    _chunk_loop(body, n_chunks)
    acc = acc_sc[...]
    o = acc[:, :hd] / acc[:, hd:]
    for h in range(n_rep):
        o_ref[:, h * hd:(h + 1) * hd] = o[h * tq:(h + 1) * tq].astype(o_ref.dtype)


def _gqa_attention(qkv, *, n_q, q_blk0, n_k, k_blk0, tq, tk, n_kv, n_rep, q_col0, k_col0, v_col0):
    hd = HEAD_DIM
    qw = n_rep * hd
    return pl.pallas_call(
        functools.partial(_gqa_kernel, tq=tq, tk=tk, n_chunks=n_k // tk, n_rep=n_rep, hd=hd),
        grid=(n_kv, n_q // tq),
        in_specs=[pl.BlockSpec((tq, qw), lambda g, i: (q_blk0 + i, q_col0 // qw + g)),
                  pl.BlockSpec((n_k, hd), lambda g, i: (k_blk0, k_col0 // hd + g)),
                  pl.BlockSpec((n_k, hd), lambda g, i: (k_blk0, v_col0 // hd + g))],
        out_specs=pl.BlockSpec((tq, qw), lambda g, i: (i, g)),
        out_shape=jax.ShapeDtypeStruct((n_q, n_kv * qw), BF16),
        scratch_shapes=[pltpu.VMEM((n_k, 2 * hd), BF16),
                        pltpu.VMEM((n_rep * tq, 1), F32),
                        pltpu.VMEM((n_rep * tq, 2 * hd), F32)],
        compiler_params=_cparams(("arbitrary", "arbitrary")),
        name="gqa_attention",
    )(qkv, qkv, qkv)


def _diff_kernel(q_ref, k_ref, v_ref, lam_ref, sub_ref, o_ref, m_sc, l_sc, acc_sc, *, tq, tk, n_chunks, hd,
                 lam_init):
    q1, q2 = q_ref[:, :hd], q_ref[:, hd:]
    m_sc[...] = jnp.full(m_sc.shape, -jnp.inf, F32)
    l_sc[...] = jnp.zeros(l_sc.shape, F32)
    acc_sc[...] = jnp.zeros(acc_sc.shape, F32)

    def body(c, carry):
        start = pl.multiple_of(c * tk, tk)
        k = k_ref[pl.ds(start, tk), :]
        v = v_ref[pl.ds(start, tk), :]
        s = jnp.concatenate([lax.dot_general(q1, k[:, :hd], NT_DIMS, preferred_element_type=F32),
                             lax.dot_general(q2, k[:, hd:], NT_DIMS, preferred_element_type=F32)], axis=0)
        rows = pl.ds(0, 2 * tq)
        p, alpha = _online_softmax_step(s, m_sc, rows)
        l_sc[...] = alpha * l_sc[...] + jnp.sum(p, axis=-1, keepdims=True)
        acc_sc[...] = alpha * acc_sc[...] + jnp.dot(p.astype(BF16), v, preferred_element_type=F32)
        return carry

    _chunk_loop(body, n_chunks)
    o = acc_sc[...] / l_sc[...]
    lp = lam_ref[...]
    lam = (jnp.exp(jnp.sum(lp[0:1] * lp[1:2], axis=-1, keepdims=True))
           - jnp.exp(jnp.sum(lp[2:3] * lp[3:4], axis=-1, keepdims=True)) + lam_init)
    o = o[:tq] - lam * o[tq:]
    o = o * lax.rsqrt(jnp.mean(o * o, axis=-1, keepdims=True) + EPS) * sub_ref[...] * (1.0 - lam_init)
    o_ref[...] = o.astype(o_ref.dtype)


def _diff_attention(qkv, lam_params, subln, *, n_q, q_blk0, n_k, k_blk0, tq, tk, n_heads,
                    q_col0, k_col0, v_col0, lam_init):
    hd = HEAD_DIM
    w2 = 2 * hd
    return pl.pallas_call(
        functools.partial(_diff_kernel, tq=tq, tk=tk, n_chunks=n_k // tk, hd=hd, lam_init=lam_init),
        grid=(n_heads, n_q // tq),
        in_specs=[pl.BlockSpec((tq, w2), lambda h, i: (q_blk0 + i, q_col0 // w2 + h)),
                  pl.BlockSpec((n_k, w2), lambda h, i: (k_blk0, k_col0 // w2 + h)),
                  pl.BlockSpec((n_k, w2), lambda h, i: (k_blk0, v_col0 // w2 + h)),
                  pl.BlockSpec((8, hd), lambda h, i: (0, 0)),
                  pl.BlockSpec((1, w2), lambda h, i: (0, 0))],
        out_specs=pl.BlockSpec((tq, w2), lambda h, i: (i, h)),
        out_shape=jax.ShapeDtypeStruct((n_q, n_heads * w2), BF16),
        scratch_shapes=[pltpu.VMEM((2 * tq, 1), F32), pltpu.VMEM((2 * tq, 1), F32),
                        pltpu.VMEM((2 * tq, w2), F32)],
        compiler_params=_cparams(("arbitrary", "arbitrary")),
        name="diff_attention",
    )(qkv, qkv, qkv, lam_params, subln)


def _compose_scan(a, b, reverse, group):
    n = a.shape[0]
    pos = lax.broadcasted_iota(jnp.int32, (n, 1), 0) & (group - 1)
    d = 1
    while d < group:
        if reverse:
            a_sh, b_sh, keep = pltpu.roll(a, n - d, 0), pltpu.roll(b, n - d, 0), pos < group - d
        else:
            a_sh, b_sh, keep = pltpu.roll(a, d, 0), pltpu.roll(b, d, 0), pos >= d
        b = jnp.where(keep, a * b_sh + b, b)
        a = jnp.where(keep, a * a_sh, a)
        d *= 2
    return a, b


def _compose_scan_sublanes(a, b, reverse):
    pos = lax.broadcasted_iota(jnp.int32, (1, SUBLANES, 1), 1)
    d = 1
    while d < SUBLANES:
        if reverse:
            a_sh, b_sh, keep = pltpu.roll(a, SUBLANES - d, 1), pltpu.roll(b, SUBLANES - d, 1), pos < SUBLANES - d
        else:
            a_sh, b_sh, keep = pltpu.roll(a, d, 1), pltpu.roll(b, d, 1), pos >= d
        b = jnp.where(keep, a * b_sh + b, b)
        a = jnp.where(keep, a * a_sh, a)
        d *= 2
    return a, b


def _scan_chunk(a, b, h0, reverse, a_sc, b_sc):
    n, c = a.shape
    ng = n // SUBLANES
    a, b = _compose_scan_sublanes(a.reshape(ng, SUBLANES, c), b.reshape(ng, SUBLANES, c), reverse)
    a, b = a.reshape(n, c), b.reshape(n, c)
    edge = 0 if reverse else SUBLANES - 1
    ag, bg = [], []
    for lb in range(c // LANES):
        a_sc[lb, 0:n, :] = a[:, lb * LANES:(lb + 1) * LANES]
        b_sc[lb, 0:n, :] = b[:, lb * LANES:(lb + 1) * LANES]
        ag.append(a_sc[lb, pl.ds(edge, ng, stride=SUBLANES), :])
        bg.append(b_sc[lb, pl.ds(edge, ng, stride=SUBLANES), :])
    ag, bg = jnp.concatenate(ag, axis=1), jnp.concatenate(bg, axis=1)
    ag, bg = _compose_scan(ag, bg, reverse, ng)
    hg = ag * h0 + bg
    grow = lax.broadcasted_iota(jnp.int32, (ng, 1), 0)
    if reverse:
        h_in = jnp.where(grow == ng - 1, h0, pltpu.roll(hg, ng - 1, 0))
        carry = hg[0:1]
    else:
        h_in = jnp.where(grow == 0, h0, pltpu.roll(hg, 1, 0))
        carry = hg[ng - 1:ng]
    h_in = jnp.broadcast_to(h_in[:, None, :], (ng, SUBLANES, c)).reshape(n, c)
    return a * h_in + b, carry


def _rglru_kernel(x_ref, gy_ref, cw_ref, cb_ref, gaw_ref, gab_ref, gxw_ref, gxb_ref, lam_ref, o_ref,
                  s_ref, a_sc, b_sc, *, n_lat, n_ctx, tc):
    n_rows = n_lat + n_ctx
    cw, cb = cw_ref[...], cb_ref[...]
    z = -lam_ref[...]
    softplus = jnp.maximum(z, 0.0) + jnp.log(1.0 + jnp.exp(-jnp.abs(z)))

    def conv(t0, n, seg0, seg1):
        xs = x_ref[pl.ds(t0, n), :]
        prev = x_ref[pl.ds(pl.multiple_of(jnp.maximum(t0 - 8, 0), 8), 8), :]
        nxt = x_ref[pl.ds(pl.multiple_of(jnp.minimum(t0 + n, n_rows - 8), 8), 8), :]
        prev = jnp.where(t0 > seg0, prev, 0.0)
        nxt = jnp.where(t0 + n < seg1, nxt, 0.0)
        ext = jnp.concatenate([prev, xs, nxt], axis=0)
        return (ext[6:6 + n] * cw[0:1] + ext[7:7 + n] * cw[1:2] + ext[8:8 + n] * cw[2:3]
                + ext[9:9 + n] * cw[3:4] + cb)

    for d in range(2):
        reverse = d == 1
        wa = gaw_ref[d, 0].astype(BF16)
        wx = gxw_ref[d, 0].astype(BF16)
        ba, bx = gab_ref[d:d + 1, :], gxb_ref[d:d + 1, :]
        c8 = -RG_C * softplus[d:d + 1, :]

        def chunk(t0, n, seg0, seg1, h0, reverse=reverse, wa=wa, wx=wx, ba=ba, bx=bx, c8=c8):
            xc = conv(t0, n, seg0, seg1)
            xb = xc.astype(BF16)
            r = jax.nn.sigmoid(jnp.dot(xb, wa, preferred_element_type=F32) + ba)
            i = jax.nn.sigmoid(jnp.dot(xb, wx, preferred_element_type=F32) + bx)
            a = jnp.exp(c8 * r)
            b = jnp.sqrt(1.0 - a * a) * (i * xc)
            return _scan_chunk(a, b, h0, reverse, a_sc, b_sc)

        _, h_end = chunk(n_lat, n_ctx, n_lat, n_rows, jnp.zeros((1, cw.shape[1]), F32))
        n_ch = n_lat // tc

        def body(ci, hc, chunk=chunk, reverse=reverse):
            cidx = (n_ch - 1 - ci) if reverse else ci
            t0 = pl.multiple_of(cidx * tc, tc)
            h, hc = chunk(t0, tc, 0, n_lat, hc)
            if reverse:
                s = s_ref[pl.ds(t0, tc), :] + h
                o_ref[pl.ds(t0, tc), :] = (s * gy_ref[pl.ds(t0, tc), :].astype(F32)).astype(o_ref.dtype)
            else:
                s_ref[pl.ds(t0, tc), :] = h
            return hc

        lax.fori_loop(0, n_ch, body, h_end)


def _rglru(xr, gy, conv_w, conv_b, ga_w, ga_b, gx_w, gx_b, lam, n_lat, n_ctx):
    n_rows, c_tot = xr.shape
    nb, bw = ga_w.shape[1], ga_w.shape[2]
    tc = _tile(n_lat, 512, 64)
    assert n_ctx % 64 == 0
    col = lambda j: (0, j)
    return pl.pallas_call(
        functools.partial(_rglru_kernel, n_lat=n_lat, n_ctx=n_ctx, tc=tc),
        grid=(nb,),
        in_specs=[pl.BlockSpec((n_rows, bw), col),
                  pl.BlockSpec((n_lat, bw), col),
                  pl.BlockSpec((conv_w.shape[0], bw), col),
                  pl.BlockSpec((1, bw), col),
                  pl.BlockSpec((2, 1, bw, bw), lambda j: (0, j, 0, 0)),
                  pl.BlockSpec((2, bw), col),
                  pl.BlockSpec((2, 1, bw, bw), lambda j: (0, j, 0, 0)),
                  pl.BlockSpec((2, bw), col),
                  pl.BlockSpec((2, bw), col)],
        out_specs=pl.BlockSpec((n_lat, bw), col),
        out_shape=jax.ShapeDtypeStruct((n_lat, c_tot), BF16),
        scratch_shapes=[pltpu.VMEM((n_lat, bw), F32),
                        pltpu.VMEM((bw // LANES, max(tc, n_ctx), LANES), F32),
                        pltpu.VMEM((bw // LANES, max(tc, n_ctx), LANES), F32)],
        compiler_params=_cparams(("arbitrary",)),
        name="rglru",
    )(xr, gy, conv_w, conv_b.reshape(1, c_tot), ga_w, ga_b, gx_w, gx_b, lam)


def _row_copy(src_hbm, row, dst, r, sem):
    return pltpu.make_async_copy(src_hbm.at[pl.ds(row, 1)], dst.at[pl.ds(r, 1)], sem)


def _gather_kernel(src_ref, nu_ref, h_hbm, o_ref, buf, sem, *, tm):
    t = pl.program_id(0)
    n_used = nu_ref[0]

    def issue(tile, slot):
        def start(r, c):
            _row_copy(h_hbm, src_ref[tile * tm + r], buf.at[slot], r, sem.at[slot]).start()
            return c
        lax.fori_loop(0, tm, start, 0, unroll=8)

    @pl.when(t == 0)
    def _():
        issue(0, 0)

    @pl.when(t + 1 < n_used)
    def _():
        issue(t + 1, (t + 1) % 2)

    @pl.when(t < n_used)
    def _():
        slot = t % 2
        pltpu.make_async_copy(h_hbm.at[pl.ds(0, tm)], buf.at[slot], sem.at[slot]).wait()
        o_ref[...] = buf[slot].astype(o_ref.dtype)

    @pl.when(t >= n_used)
    def _():
        o_ref[...] = jnp.zeros(o_ref.shape, o_ref.dtype)


def _moe_gather(h, src_tok, n_used, n_rows, tm):
    d = h.shape[1]
    return pl.pallas_call(
        functools.partial(_gather_kernel, tm=tm),
        grid_spec=pltpu.PrefetchScalarGridSpec(
            num_scalar_prefetch=2,
            grid=(n_rows // tm,),
            in_specs=[pl.BlockSpec(memory_space=pl.ANY)],
            out_specs=pl.BlockSpec((tm, d), lambda t, src, nu: (t, 0)),
            scratch_shapes=[pltpu.VMEM((2, tm, d), F32), pltpu.SemaphoreType.DMA((2,))]),
        out_shape=jax.ShapeDtypeStruct((n_rows, d), BF16),
        compiler_params=_cparams(("arbitrary",)),
        name="moe_gather",
    )(src_tok, n_used, h)


def _expert_rows_kernel(ts_ref, nt_ref, nu_ref, x_hbm, *refs, n_w, tm, tn, compute):
    w_refs, o_hbm = refs[:n_w], refs[n_w]
    wb = refs[n_w + 1:2 * n_w + 1]
    xbuf, obuf, sem_in, sem_out = refs[2 * n_w + 1:]
    j, e = pl.program_id(0), pl.program_id(1)
    nt = nt_ref[e]
    row0 = ts_ref[e] * tm
    col0 = pl.multiple_of(j * tn, tn)

    tu = 2 * tm
    n_full = nt // 2
    tail = nt - 2 * n_full

    def in_copy(u, rows, slot):
        r = pl.multiple_of(row0 + u * tu, tm)
        return pltpu.make_async_copy(x_hbm.at[pl.ds(r, rows)], xbuf.at[slot, pl.ds(0, rows)], sem_in.at[slot])

    def out_copy(u, rows, slot):
        r = pl.multiple_of(row0 + u * tu, tm)
        return pltpu.make_async_copy(obuf.at[slot, pl.ds(0, rows)], o_hbm.at[pl.ds(r, rows), pl.ds(col0, tn)],
                                     sem_out.at[slot])

    def run(rows, slot):
        x = xbuf[slot, pl.ds(0, rows), :]
        obuf[slot, pl.ds(0, rows), :] = compute(x, *[w_b[...] for w_b in wb]).astype(obuf.dtype)

    def convert_weights():
        for w_ref, w_b in zip(w_refs, wb):
            w_b[...] = w_ref[0].astype(BF16)

    @pl.when(n_full > 0)
    def _():
        in_copy(0, tu, 0).start()
        convert_weights()

        def body(u, c):
            slot = u % 2

            @pl.when(u + 1 < n_full)
            def _():
                in_copy(u + 1, tu, 1 - slot).start()

            in_copy(u, tu, slot).wait()

            @pl.when(u >= 2)
            def _():
                out_copy(u - 2, tu, slot).wait()

            run(tu, slot)
            out_copy(u, tu, slot).start()
            return c

        lax.fori_loop(0, n_full, body, 0)

        @pl.when(n_full >= 2)
        def _():
            out_copy(n_full - 2, tu, n_full % 2).wait()
        out_copy(n_full - 1, tu, (n_full - 1) % 2).wait()

    @pl.when(tail > 0)
    def _():
        cp_in = in_copy(n_full, tm, 0)
        cp_in.start()

        @pl.when(n_full == 0)
        def _():
            convert_weights()

        cp_in.wait()
        run(tm, 0)
        cp_out = out_copy(n_full, tm, 0)
        cp_out.start()
        cp_out.wait()

    @pl.when(e == pl.num_programs(1) - 1)
    def _():
        obuf[0] = jnp.zeros(obuf.shape[1:], obuf.dtype)

        def zero_tile(t, c):
            cp = pltpu.make_async_copy(obuf.at[0, pl.ds(0, tm)],
                                       o_hbm.at[pl.ds(pl.multiple_of(t * tm, tm), tm), pl.ds(col0, tn)],
                                       sem_out.at[0])
            cp.start()
            cp.wait()
            return c

        lax.fori_loop(nu_ref[0], o_hbm.shape[0] // tm, zero_tile, 0)


def _swiglu_tile(x, wg, wu):
    g = jnp.dot(x, wg, preferred_element_type=F32)
    u = jnp.dot(x, wu, preferred_element_type=F32)
    return g * jax.nn.sigmoid(g) * u


def _down_tile(x, wd):
    return jnp.dot(x, wd, preferred_element_type=F32)


def _moe_expert_matmul(tile_start, n_tiles, n_used, xs, weights, compute, out_dtype, tm, tn, name):
    n_rows, k = xs.shape
    n_exp, _, n_out = weights[0].shape
    n_w = len(weights)
    wspec = pl.BlockSpec((1, k, tn), lambda j, e, ts, nt, nu: (e, 0, j))
    any_spec = pl.BlockSpec(memory_space=pl.ANY)
    return pl.pallas_call(
        functools.partial(_expert_rows_kernel, n_w=n_w, tm=tm, tn=tn, compute=compute),
        grid_spec=pltpu.PrefetchScalarGridSpec(
            num_scalar_prefetch=3,
            grid=(n_out // tn, n_exp),
            in_specs=[any_spec] + [wspec] * n_w,
            out_specs=any_spec,
            scratch_shapes=[pltpu.VMEM((k, tn), BF16)] * n_w + [
                pltpu.VMEM((2, 2 * tm, k), xs.dtype), pltpu.VMEM((2, 2 * tm, tn), out_dtype),
                pltpu.SemaphoreType.DMA((2,)), pltpu.SemaphoreType.DMA((2,))]),
        out_shape=jax.ShapeDtypeStruct((n_rows, n_out), out_dtype),
        compiler_params=_cparams(("arbitrary", "arbitrary")),
        name=name,
    )(tile_start, n_tiles, n_used, xs, *weights)


def _combine_kernel(dest_ref, x_ref, g_ref, w_ref, y_hbm, o_ref, buf, sem, *, tt):
    i = pl.program_id(0)

    def issue(tile, slot):
        def start(r, c):
            tok = tile * tt + r
            for kk in range(TOP_K):
                _row_copy(y_hbm, dest_ref[TOP_K * tok + kk], buf.at[slot, kk], r, sem.at[slot]).start()
            return c
        lax.fori_loop(0, tt, start, 0, unroll=4)

    @pl.when(i == 0)
    def _():
        issue(0, 0)

    @pl.when(i + 1 < pl.num_programs(0))
    def _():
        issue(i + 1, (i + 1) % 2)

    slot = i % 2

    for kk in range(TOP_K):
        pltpu.make_async_copy(y_hbm.at[pl.ds(0, tt)], buf.at[slot, kk], sem.at[slot]).wait()
    w = w_ref[...]
    mix = w[:, 0:1] * buf[slot, 0] + w[:, 1:2] * buf[slot, 1]
    o_ref[...] = x_ref[...] + g_ref[0:1, :] * mix


def _moe_combine(dest, x, gate, w_top, y, n_tok, tt):
    d = x.shape[1]
    return pl.pallas_call(
        functools.partial(_combine_kernel, tt=tt),
        grid_spec=pltpu.PrefetchScalarGridSpec(
            num_scalar_prefetch=1,
            grid=(n_tok // tt,),
            in_specs=[pl.BlockSpec((tt, d), lambda i, dst: (i, 0)),
                      pl.BlockSpec((2, d), lambda i, dst: (0, 0)),
                      pl.BlockSpec((tt, LANES), lambda i, dst: (i, 0)),
                      pl.BlockSpec(memory_space=pl.ANY)],
            out_specs=pl.BlockSpec((tt, d), lambda i, dst: (i, 0)),
            scratch_shapes=[pltpu.VMEM((2, TOP_K, tt, d), F32), pltpu.SemaphoreType.DMA((2,))]),
        out_shape=jax.ShapeDtypeStruct((n_tok, d), F32),
        compiler_params=_cparams(("arbitrary",)),
        name="moe_combine",
    )(dest, x, gate, w_top, y)


def _moe_plan(e_idx, n_exp, tm):
    n_tok = e_idx.shape[0]
    n_slot = n_tok * TOP_K
    flat_e = e_idx.reshape(n_slot)
    onehot = (flat_e[:, None] == jnp.arange(n_exp, dtype=jnp.int32)[None, :]).astype(jnp.int32)
    csum = jnp.cumsum(onehot, axis=0)
    rank = jnp.sum(csum * onehot, axis=1) - 1
    counts = csum[-1]
    n_tiles = (counts + tm - 1) // tm
    tile_end = jnp.cumsum(n_tiles)
    tile_start = tile_end - n_tiles
    dest = (tile_start[flat_e] * tm + rank).astype(jnp.int32)
    n_rows = (n_slot // tm + n_exp) * tm
    src_tok = jnp.zeros((n_rows,), jnp.int32).at[dest].set(jnp.arange(n_slot, dtype=jnp.int32) // TOP_K)
    return dest, src_tok, tile_start.astype(jnp.int32), n_tiles.astype(jnp.int32), tile_end[-1:].astype(jnp.int32), n_rows


def _rope_tables(n_lat, n_ctx):
    half = HEAD_DIM // 2
    inv = ROPE_THETA ** (-jnp.arange(0, half, 2, dtype=F32) / half)
    rows = n_lat // GRID_W
    row = jnp.repeat(jnp.arange(rows, dtype=F32), GRID_W)
    col = jnp.tile(jnp.arange(GRID_W, dtype=F32), rows)
    ang = jnp.concatenate([row[:, None] * inv, col[:, None] * inv], axis=-1)
    cos = jnp.repeat(jnp.cos(ang), 2, axis=-1)
    sin = jnp.repeat(jnp.sin(ang), 2, axis=-1)
    even = (jnp.arange(HEAD_DIM) % 2 == 0)[None, :]
    sa = jnp.where(even, -sin, 0.0)
    sb = jnp.where(even, 0.0, sin)
    ident = jnp.ones((n_ctx, HEAD_DIM), F32)
    zero = jnp.zeros((n_ctx, HEAD_DIM), F32)
    return (jnp.concatenate([cos, ident]), jnp.concatenate([sa, zero]), jnp.concatenate([sb, zero]))


def kernel(x, c, ctx, c_ctx, attn_w_mod, attn_b_mod, attn_norm_mix, attn_norm_ffn, attn_w_in, attn_gqa_q_norm, attn_gqa_k_norm, attn_diff_q_norm, attn_diff_k_norm, attn_diff_lambda_q1, attn_diff_lambda_k1, attn_diff_lambda_q2, attn_diff_lambda_k2, attn_diff_subln, attn_w_out, ffn_w_gate, ffn_w_up, ffn_w_down, rec_w_mod, rec_b_mod, rec_norm_mix, rec_norm_ffn, rec_w_in, rec_conv_w, rec_conv_b, rec_gate_a_w, rec_gate_a_b, rec_gate_x_w, rec_gate_x_b, rec_lru_lambda, rec_w_out, moe_router_w, moe_router_b, moe_w_gate, moe_w_up, moe_w_down):
    assert x.shape[0] == 1 and attn_w_in.shape[0] == 1 and rec_w_in.shape[0] == 1
    n_lat, d = x.shape[1], x.shape[2]
    n_ctx = ctx.shape[1]
    n_rows = n_lat + n_ctx
    hd = HEAD_DIM
    gqa_heads, diff_heads = d // 256, d // 512
    kv_heads = gqa_heads // 4
    n_rep = gqa_heads // kv_heads
    gq_w, gkv_w = gqa_heads * hd, kv_heads * hd
    dq_w, dv_w = diff_heads * 2 * hd, diff_heads * 2 * hd
    col_ka, col_va, col_qb = gq_w, gq_w + gkv_w, gq_w + 2 * gkv_w
    col_kb, col_vb = col_qb + dq_w, col_qb + 2 * dq_w
    d_ff = ffn_w_gate.shape[2]
    n_exp = moe_router_w.shape[2]

    xs = jnp.concatenate([x[0], ctx[0]], axis=0)
    c2 = jnp.zeros((8, d), F32).at[0].set(c[0]).at[1].set(c_ctx)
    tm_big = _tile(n_rows, 1056, 16)

    mods = _modvec(c2, attn_w_mod[0], attn_b_mod[0])[:2].reshape(2, 6, d)
    sh1, sc1, g1, sh2, sc2, g2 = (mods[:, j] for j in range(6))
    h = _normmod(xs, attn_norm_mix[0], sh1, sc1, n_lat)

    scale = hd ** -0.5 * math.log2(math.e)
    tn_qkv = min(4 * hd, gkv_w)
    gains = jnp.concatenate([
        jnp.tile(attn_gqa_q_norm[0] * scale, gqa_heads), jnp.tile(attn_gqa_k_norm[0], kv_heads),
        jnp.ones((gkv_w,), F32),
        jnp.tile((attn_diff_q_norm[0] * scale).reshape(-1), diff_heads),
        jnp.tile(attn_diff_k_norm[0].reshape(-1), diff_heads), jnp.ones((dv_w,), F32)]).reshape(1, -1)
    v_tiles = tuple(range(col_va // tn_qkv, col_qb // tn_qkv)) + tuple(
        range(col_vb // tn_qkv, (col_vb + dv_w) // tn_qkv))
    cos, sa, sb = _rope_tables(n_lat, n_ctx)
    qkv = _qkv(h, attn_w_in[0], gains, cos, sa, sb, v_tiles, tm_big, tn_qkv)

    tq_ctx = n_ctx
    tq_gqa = _tile(n_lat, 512, 16)
    tq_diff = _tile(n_lat, 512, 16)
    tk_lat = _tile(n_rows, 1536, LANES)
    layer = 0
    lam_init = 0.8 - 0.6 * math.exp(-0.3 * layer)
    lam_params = jnp.zeros((8, hd), F32).at[0].set(attn_diff_lambda_q1[0]).at[1].set(attn_diff_lambda_k1[0]) \
        .at[2].set(attn_diff_lambda_q2[0]).at[3].set(attn_diff_lambda_k2[0])
    subln = attn_diff_subln[0].reshape(1, 2 * hd)
    gqa_args = dict(n_kv=kv_heads, n_rep=n_rep, q_col0=0, k_col0=col_ka, v_col0=col_va)
    diff_args = dict(n_heads=diff_heads, q_col0=col_qb, k_col0=col_kb, v_col0=col_vb, lam_init=lam_init)
    lat_args = dict(n_q=n_lat, q_blk0=0, n_k=n_rows, k_blk0=0, tk=tk_lat)
    ctx_args = dict(n_q=n_ctx, q_blk0=n_lat // tq_ctx, n_k=n_ctx, k_blk0=n_lat // n_ctx, tk=n_ctx, tq=tq_ctx)
    oa = jnp.concatenate([_gqa_attention(qkv, tq=tq_gqa, **lat_args, **gqa_args),
                          _gqa_attention(qkv, **ctx_args, **gqa_args)])
    ob = jnp.concatenate([_diff_attention(qkv, lam_params, subln, tq=tq_diff, **lat_args, **diff_args),
                          _diff_attention(qkv, lam_params, subln, **ctx_args, **diff_args)])

    x1 = _mm([(oa, gq_w), (ob, dv_w)], attn_w_out[0], m=n_rows, tm=tm_big, tn=512, n_out=d, out_dtype=F32,
             resid=xs, gate=g1, n_lat=n_lat, name="attn_out")
    h = _normmod(x1, attn_norm_ffn[0], sh2, sc2, n_lat)
    hff = _gateup(h, ffn_w_gate[0], ffn_w_up[0], tm_big, 256)
    k_half = d_ff // 2
    x2 = x1
    for kb in range(2):
        x2 = _mm([(hff, k_half)], ffn_w_down[0], m=n_rows, tm=tm_big, tn=256, n_out=d, out_dtype=F32,
                 w_row_blk=kb, a_col_blk=[kb], resid=x2, gate=g2, n_lat=n_lat, name="ffn_down")

    mods = _modvec(c2, rec_w_mod[0], rec_b_mod[0])[:2].reshape(2, 6, d)
    sh1, sc1, g1, sh2, sc2, g2 = (mods[:, j] for j in range(6))
    h = _normmod(x2, rec_norm_mix[0], sh1, sc1, n_lat)
    d_rnn = rec_w_out.shape[1]
    gy = _mm([(h, d)], rec_w_in[0], m=n_rows, tm=tm_big, tn=512, n_out=d_rnn, out_dtype=BF16,
             act="gelu", name="rec_in_y")
    xr = _mm([(h, d)], rec_w_in[0], m=n_rows, tm=tm_big, tn=512, n_out=d_rnn, out_dtype=F32,
             w_col0=d_rnn, name="rec_in_x")
    s = _rglru(xr, gy, rec_conv_w[0], rec_conv_b[0], rec_gate_a_w[0], rec_gate_a_b[0],
               rec_gate_x_w[0], rec_gate_x_b[0], rec_lru_lambda[0], n_lat, n_ctx)
    tm_lat = _tile(n_lat, 1024, 16)
    x3 = _mm([(s, d_rnn)], rec_w_out[0], m=n_lat, tm=tm_lat, tn=512, n_out=d, out_dtype=F32,
             resid=x2, gate=g1, n_lat=n_lat, name="rec_out")

    h4, w_top, e_top = _normmod_router(x3, rec_norm_ffn[0], sh2, sc2, moe_router_w[0], moe_router_b[0], n_lat)
    tm_e = 256
    dest, src_tok, tile_start, n_tiles, n_used, n_sorted = _moe_plan(e_top[:, :TOP_K], n_exp, tm_e)
    xg = _moe_gather(h4, src_tok, n_used, n_sorted, tm_e)
    hs = _moe_expert_matmul(tile_start, n_tiles, n_used, xg, [moe_w_gate[0], moe_w_up[0]], _swiglu_tile, BF16,
                            tm_e, 512, "moe_gateup")
    ys = _moe_expert_matmul(tile_start, n_tiles, n_used, hs, [moe_w_down[0]], _down_tile, F32, tm_e,
                            _tile(d, 1024, LANES), "moe_down")
    out = _moe_combine(dest, x3, g2, w_top, ys, n_lat, 256)
    return out.reshape(1, n_lat, d)
```

```python
import functools
import math

import jax
import jax.numpy as jnp
from jax import lax
from jax.experimental import pallas as pl
from jax.experimental.pallas import tpu as pltpu

F32 = jnp.float32
BF16 = jnp.bfloat16
EPS = 1e-6
ROPE_THETA = 10000.0
GRID_W = 64
HEAD_DIM = 128
RG_C = 8.0
TOP_K = 2
LANES = 128
SUBLANES = 8
VMEM_LIMIT = 56 * 1024 * 1024
VMEM_LIMIT_MOE = 58 * 1024 * 1024
NT_DIMS = (((1,), (1,)), ((), ()))


def _cparams(sem, vmem=VMEM_LIMIT):
    return pltpu.CompilerParams(dimension_semantics=sem, vmem_limit_bytes=vmem)


def _tile(n, target, mult):
    best = None
    for t in range(mult, min(n, target) + 1, mult):
        if n % t == 0:
            best = t
    assert best is not None, (n, target, mult)
    return best


def _row_select(row0, tm, n_lat, ref):
    row = row0 + lax.broadcasted_iota(jnp.int32, (tm, 1), 0)
    return jnp.where(row < n_lat, ref[0:1, :], ref[1:2, :])


def _modvec_kernel(c_ref, w_ref, b_ref, o_ref):
    c = c_ref[...]
    a = (c * jax.nn.sigmoid(c)).astype(BF16)
    o_ref[...] = jnp.dot(a, w_ref[...].astype(BF16), preferred_element_type=F32) + b_ref[...]


def _modvec(c2, w_mod, b_mod):
    d, n = w_mod.shape
    tn = _tile(n, 512, LANES)
    return pl.pallas_call(
        _modvec_kernel,
        grid=(n // tn,),
        in_specs=[pl.BlockSpec((8, d), lambda j: (0, 0)),
                  pl.BlockSpec((d, tn), lambda j: (0, j)),
                  pl.BlockSpec((1, tn), lambda j: (0, j))],
        out_specs=pl.BlockSpec((8, tn), lambda j: (0, j)),
        out_shape=jax.ShapeDtypeStruct((8, n), F32),
        compiler_params=_cparams(("arbitrary",)),
        name="modvec",
    )(c2, w_mod, b_mod.reshape(1, n))


def _normmod_kernel(x_ref, g_ref, sh_ref, sc_ref, o_ref, *, tm, n_lat):
    x = x_ref[...]
    y = x * lax.rsqrt(jnp.mean(x * x, axis=-1, keepdims=True) + EPS) * g_ref[...]
    row0 = pl.program_id(0) * tm
    sc = _row_select(row0, tm, n_lat, sc_ref)
    sh = _row_select(row0, tm, n_lat, sh_ref)
    o_ref[...] = (y * (1.0 + sc) + sh).astype(o_ref.dtype)


def _normmod(x, g, sh, sc, n_lat, m=None):
    m = x.shape[0] if m is None else m
    d = x.shape[1]
    tm = _tile(m, 528, 16)
    vec = pl.BlockSpec((1, d), lambda i: (0, 0))
    two = pl.BlockSpec((2, d), lambda i: (0, 0))
    return pl.pallas_call(
        functools.partial(_normmod_kernel, tm=tm, n_lat=n_lat),
        grid=(m // tm,),
        in_specs=[pl.BlockSpec((tm, d), lambda i: (i, 0)), vec, two, two],
        out_specs=pl.BlockSpec((tm, d), lambda i: (i, 0)),
        out_shape=jax.ShapeDtypeStruct((m, d), BF16),
        compiler_params=_cparams(("arbitrary",)),
        name="normmod",
    )(x, g.reshape(1, d), sh, sc)


def _normmod_router_kernel(x_ref, g_ref, sh_ref, sc_ref, rw_ref, rb_ref, h_ref, w_ref, e_ref, *, n_exp):
    x = x_ref[...]
    y = x * lax.rsqrt(jnp.mean(x * x, axis=-1, keepdims=True) + EPS) * g_ref[...]
    h = y * (1.0 + sc_ref[0:1, :]) + sh_ref[0:1, :]
    h_ref[...] = h
    logits = jnp.dot(h, rw_ref[...], preferred_element_type=F32,
                     precision=lax.Precision.HIGHEST) + rb_ref[...]
    lane = lax.broadcasted_iota(jnp.int32, logits.shape, 1)
    lanef = lane.astype(F32)
    neg = jnp.float32(-jnp.inf)
    lg = jnp.where(lane < n_exp, logits, neg)
    m1 = jnp.max(lg, axis=-1, keepdims=True)
    i1 = jnp.min(jnp.where(lg == m1, lanef, float(LANES)), axis=-1, keepdims=True)
    lg2 = jnp.where(lanef == i1, neg, lg)
    m2 = jnp.max(lg2, axis=-1, keepdims=True)
    i2 = jnp.min(jnp.where(lg2 == m2, lanef, float(LANES)), axis=-1, keepdims=True)
    e = jnp.exp(m2 - m1)
    w1 = 1.0 / (1.0 + e)
    w2 = e / (1.0 + e)
    w_ref[...] = jnp.where(lane == 0, w1, jnp.where(lane == 1, w2, 0.0))
    e_ref[...] = jnp.where(lane == 0, i1, jnp.where(lane == 1, i2, 0.0)).astype(jnp.int32)


def _normmod_router(x, g, sh, sc, router_w, router_b, m):
    d = x.shape[1]
    n_exp = router_w.shape[1]
    tm = _tile(m, 256, 8)
    rw = jnp.zeros((d, LANES), F32).at[:, :n_exp].set(router_w)
    rb = jnp.zeros((1, LANES), F32).at[0, :n_exp].set(router_b)
    vec = pl.BlockSpec((1, d), lambda i: (0, 0))
    two = pl.BlockSpec((2, d), lambda i: (0, 0))
    return pl.pallas_call(
        functools.partial(_normmod_router_kernel, n_exp=n_exp),
        grid=(m // tm,),
        in_specs=[pl.BlockSpec((tm, d), lambda i: (i, 0)), vec, two, two,
                  pl.BlockSpec((d, LANES), lambda i: (0, 0)),
                  pl.BlockSpec((1, LANES), lambda i: (0, 0))],
        out_specs=[pl.BlockSpec((tm, d), lambda i: (i, 0)),
                   pl.BlockSpec((tm, LANES), lambda i: (i, 0)),
                   pl.BlockSpec((tm, LANES), lambda i: (i, 0))],
        out_shape=[jax.ShapeDtypeStruct((m, d), F32),
                   jax.ShapeDtypeStruct((m, LANES), F32),
                   jax.ShapeDtypeStruct((m, LANES), jnp.int32)],
        compiler_params=_cparams(("arbitrary",)),
        name="normmod_router",
    )(x, g.reshape(1, d), sh, sc, rw, rb)


def _mm_kernel(*refs, ks, act, has_res, tm, n_lat):
    n_a = len(ks)
    a_refs, w_ref = refs[:n_a], refs[n_a]
    o_ref = refs[-1]
    w = w_ref[...].astype(BF16)
    acc, off = None, 0
    for a_ref, k in zip(a_refs, ks):
        part = jnp.dot(a_ref[...], w[off:off + k], preferred_element_type=F32)
        acc = part if acc is None else acc + part
        off += k
    if act == "gelu":
        acc = jax.nn.gelu(acc, approximate=True)
    if has_res:
        x_ref, gate_ref = refs[n_a + 1], refs[n_a + 2]
        gate = _row_select(pl.program_id(0) * tm, tm, n_lat, gate_ref)
        acc = x_ref[...] + gate * acc
    o_ref[...] = acc.astype(o_ref.dtype)


def _mm(a_list, w, *, m, tm, tn, n_out, out_dtype, w_row_blk=0, w_col0=0, a_col_blk=None,
        act=None, resid=None, gate=None, n_lat=0, name="mm"):
    ks = tuple(k for _, k in a_list)
    arrs = [a for a, _ in a_list]
    kt = sum(ks)
    a_col_blk = a_col_blk or [0] * len(arrs)
    assert m % tm == 0 and n_out % tn == 0 and w_col0 % tn == 0
    c0 = w_col0 // tn
    in_specs = [pl.BlockSpec((tm, k), functools.partial(lambda i, j, cb: (i, cb), cb=cb))
                for k, cb in zip(ks, a_col_blk)]
    in_specs.append(pl.BlockSpec((kt, tn), lambda i, j: (w_row_blk, c0 + j)))
    args = arrs + [w]
    if resid is not None:
        in_specs += [pl.BlockSpec((tm, tn), lambda i, j: (i, j)),
                     pl.BlockSpec((2, tn), lambda i, j: (0, j))]
        args += [resid, gate]
    return pl.pallas_call(
        functools.partial(_mm_kernel, ks=ks, act=act, has_res=resid is not None, tm=tm, n_lat=n_lat),
        grid=(m // tm, n_out // tn),
        in_specs=in_specs,
        out_specs=pl.BlockSpec((tm, tn), lambda i, j: (i, j)),
        out_shape=jax.ShapeDtypeStruct((m, n_out), out_dtype),
        compiler_params=_cparams(("arbitrary", "arbitrary")),
        name=name,
    )(*args)


def _qkv_kernel(a_ref, w_ref, g_ref, cos_ref, sa_ref, sb_ref, o_ref, ybuf, *, v_tiles, hd):
    i, j = pl.program_id(0), pl.program_id(1)

    @pl.when(jnp.logical_and(i == 0, j == 0))
    def _():
        ybuf[...] = jnp.zeros(ybuf.shape, ybuf.dtype)

    is_v = functools.reduce(jnp.logical_or, [j - 1 == t for t in v_tiles])

    def step(new, old):
        y = ybuf[old]
        ybuf[new] = jnp.dot(a_ref[...], w_ref[...].astype(BF16), preferred_element_type=F32)
        cos, sa, sb = cos_ref[...], sa_ref[...], sb_ref[...]
        for h in range(y.shape[1] // hd):
            yh = y[:, h * hd:(h + 1) * hd]
            yn = yh * lax.rsqrt(jnp.mean(yh * yh, axis=-1, keepdims=True) + EPS) * g_ref[:, h * hd:(h + 1) * hd]
            out = yn * cos + pltpu.roll(yn, hd - 1, 1) * sa + pltpu.roll(yn, 1, 1) * sb
            o_ref[:, h * hd:(h + 1) * hd] = jnp.where(is_v, yh, out).astype(o_ref.dtype)

    @pl.when(j % 2 == 0)
    def _():
        step(0, 1)

    @pl.when(j % 2 == 1)
    def _():
        step(1, 0)


def _qkv(h, w_in, gains, cos, sa, sb, v_tiles, tm, tn):
    m, d = h.shape
    n = w_in.shape[1]
    nt = n // tn
    tab = pl.BlockSpec((tm, HEAD_DIM), lambda i, j: (i, 0))
    prev = lambda i, j: (0, jnp.maximum(j - 1, 0))
    return pl.pallas_call(
        functools.partial(_qkv_kernel, v_tiles=v_tiles, hd=HEAD_DIM),
        grid=(m // tm, nt + 1),
        in_specs=[pl.BlockSpec((tm, d), lambda i, j: (i, 0)),
                  pl.BlockSpec((d, tn), lambda i, j: (0, jnp.minimum(j, nt - 1))),
                  pl.BlockSpec((1, tn), prev),
                  tab, tab, tab],
        out_specs=pl.BlockSpec((tm, tn), lambda i, j: (i, jnp.maximum(j - 1, 0))),
        out_shape=jax.ShapeDtypeStruct((m, n), BF16),
        scratch_shapes=[pltpu.VMEM((2, tm, tn), F32)],
        compiler_params=_cparams(("arbitrary", "arbitrary")),
        name="qkv_proj",
    )(h, w_in, gains, cos, sa, sb)


def _gateup_kernel(a_ref, wg_ref, wu_ref, o_ref):
    a = a_ref[...]
    g = jnp.dot(a, wg_ref[...].astype(BF16), preferred_element_type=F32)
    u = jnp.dot(a, wu_ref[...].astype(BF16), preferred_element_type=F32)
    o_ref[...] = (g * jax.nn.sigmoid(g) * u).astype(o_ref.dtype)


def _gateup(h, wg, wu, tm, tn):
    m, d = h.shape
    n = wg.shape[1]
    wspec = pl.BlockSpec((d, tn), lambda i, j: (0, j))
    return pl.pallas_call(
        _gateup_kernel,
        grid=(m // tm, n // tn),
        in_specs=[pl.BlockSpec((tm, d), lambda i, j: (i, 0)), wspec, wspec],
        out_specs=pl.BlockSpec((tm, tn), lambda i, j: (i, j)),
        out_shape=jax.ShapeDtypeStruct((m, n), BF16),
        compiler_params=_cparams(("arbitrary", "arbitrary")),
        name="ffn_gateup",
    )(h, wg, wu)


def _online_softmax_step(s, m_sc, rows):
    m_old = m_sc[rows, :]
    m_new = jnp.maximum(m_old, jnp.max(s, axis=-1, keepdims=True))
    m_sc[rows, :] = m_new
    return jnp.exp2(s - m_new), jnp.exp2(m_old - m_new)


CHAIN_HEADS = 1


def _chunk_loop(body, n_chunks):
    lax.fori_loop(0, n_chunks, body, 0, unroll=2 if n_chunks % 2 == 0 else 1)


def _gqa_kernel(q_ref, k_ref, v_ref, o_ref, vaug, m_sc, acc_sc, *, tq, tk, n_chunks, n_rep, hd):
    @pl.when(pl.program_id(1) == 0)
    def _():
        vaug[:, :hd] = v_ref[...]
        vaug[:, hd:] = jnp.ones((vaug.shape[0], hd), vaug.dtype)

    q = jnp.concatenate([q_ref[:, h * hd:(h + 1) * hd] for h in range(n_rep)], axis=0)
    m_sc[...] = jnp.full(m_sc.shape, -jnp.inf, F32)
    acc_sc[...] = jnp.zeros(acc_sc.shape, F32)

    def body(c, carry):
        start = pl.multiple_of(c * tk, tk)
        k = k_ref[pl.ds(start, tk), :]
        v = vaug[pl.ds(start, tk), :]
        cr = min(CHAIN_HEADS, n_rep) * tq
        for r0 in range(0, n_rep * tq, cr):
            rows = pl.ds(r0, cr)
            s = lax.dot_general(q[r0:r0 + cr], k, NT_DIMS, preferred_element_type=F32)
            p, alpha = _online_softmax_step(s, m_sc, rows)
            acc_sc[rows, :] = alpha * acc_sc[rows, :] + jnp.dot(p.astype(BF16), v, preferred_element_type=F32)
        return carry

    _chunk_loop(body, n_chunks)
    acc = acc_sc[...]
    o = acc[:, :hd] / acc[:, hd:]
    for h in range(n_rep):
        o_ref[:, h * hd:(h + 1) * hd] = o[h * tq:(h + 1) * tq].astype(o_ref.dtype)


def _gqa_attention(qkv, *, n_q, q_blk0, n_k, k_blk0, tq, tk, n_kv, n_rep, q_col0, k_col0, v_col0):
    hd = HEAD_DIM
    qw = n_rep * hd
    return pl.pallas_call(
        functools.partial(_gqa_kernel, tq=tq, tk=tk, n_chunks=n_k // tk, n_rep=n_rep, hd=hd),
        grid=(n_kv, n_q // tq),
        in_specs=[pl.BlockSpec((tq, qw), lambda g, i: (q_blk0 + i, q_col0 // qw + g)),
                  pl.BlockSpec((n_k, hd), lambda g, i: (k_blk0, k_col0 // hd + g)),
                  pl.BlockSpec((n_k, hd), lambda g, i: (k_blk0, v_col0 // hd + g))],
        out_specs=pl.BlockSpec((tq, qw), lambda g, i: (i, g)),
        out_shape=jax.ShapeDtypeStruct((n_q, n_kv * qw), BF16),
        scratch_shapes=[pltpu.VMEM((n_k, 2 * hd), BF16),
                        pltpu.VMEM((n_rep * tq, 1), F32),
                        pltpu.VMEM((n_rep * tq, 2 * hd), F32)],
        compiler_params=_cparams(("arbitrary", "arbitrary")),
        name="gqa_attention",
    )(qkv, qkv, qkv)


def _diff_kernel(q_ref, k_ref, v_ref, lam_ref, sub_ref, o_ref, m_sc, l_sc, acc_sc, *, tq, tk, n_chunks, hd,
                 lam_init):
    q1, q2 = q_ref[:, :hd], q_ref[:, hd:]
    m_sc[...] = jnp.full(m_sc.shape, -jnp.inf, F32)
    l_sc[...] = jnp.zeros(l_sc.shape, F32)
    acc_sc[...] = jnp.zeros(acc_sc.shape, F32)

    def body(c, carry):
        start = pl.multiple_of(c * tk, tk)
        k = k_ref[pl.ds(start, tk), :]
        v = v_ref[pl.ds(start, tk), :]
        s = jnp.concatenate([lax.dot_general(q1, k[:, :hd], NT_DIMS, preferred_element_type=F32),
                             lax.dot_general(q2, k[:, hd:], NT_DIMS, preferred_element_type=F32)], axis=0)
        rows = pl.ds(0, 2 * tq)
        p, alpha = _online_softmax_step(s, m_sc, rows)
        l_sc[...] = alpha * l_sc[...] + jnp.sum(p, axis=-1, keepdims=True)
        acc_sc[...] = alpha * acc_sc[...] + jnp.dot(p.astype(BF16), v, preferred_element_type=F32)
        return carry

    _chunk_loop(body, n_chunks)
    o = acc_sc[...] / l_sc[...]
    lp = lam_ref[...]
    lam = (jnp.exp(jnp.sum(lp[0:1] * lp[1:2], axis=-1, keepdims=True))
           - jnp.exp(jnp.sum(lp[2:3] * lp[3:4], axis=-1, keepdims=True)) + lam_init)
    o = o[:tq] - lam * o[tq:]
    o = o * lax.rsqrt(jnp.mean(o * o, axis=-1, keepdims=True) + EPS) * sub_ref[...] * (1.0 - lam_init)
    o_ref[...] = o.astype(o_ref.dtype)


def _diff_attention(qkv, lam_params, subln, *, n_q, q_blk0, n_k, k_blk0, tq, tk, n_heads,
                    q_col0, k_col0, v_col0, lam_init):
    hd = HEAD_DIM
    w2 = 2 * hd
    return pl.pallas_call(
        functools.partial(_diff_kernel, tq=tq, tk=tk, n_chunks=n_k // tk, hd=hd, lam_init=lam_init),
        grid=(n_heads, n_q // tq),
        in_specs=[pl.BlockSpec((tq, w2), lambda h, i: (q_blk0 + i, q_col0 // w2 + h)),
                  pl.BlockSpec((n_k, w2), lambda h, i: (k_blk0, k_col0 // w2 + h)),
                  pl.BlockSpec((n_k, w2), lambda h, i: (k_blk0, v_col0 // w2 + h)),
                  pl.BlockSpec((8, hd), lambda h, i: (0, 0)),
                  pl.BlockSpec((1, w2), lambda h, i: (0, 0))],
        out_specs=pl.BlockSpec((tq, w2), lambda h, i: (i, h)),
        out_shape=jax.ShapeDtypeStruct((n_q, n_heads * w2), BF16),
        scratch_shapes=[pltpu.VMEM((2 * tq, 1), F32), pltpu.VMEM((2 * tq, 1), F32),
                        pltpu.VMEM((2 * tq, w2), F32)],
        compiler_params=_cparams(("arbitrary", "arbitrary")),
        name="diff_attention",
    )(qkv, qkv, qkv, lam_params, subln)


def _compose_scan(a, b, reverse, group):
    n = a.shape[0]
    pos = lax.broadcasted_iota(jnp.int32, (n, 1), 0) & (group - 1)
    d = 1
    while d < group:
        if reverse:
            a_sh, b_sh, keep = pltpu.roll(a, n - d, 0), pltpu.roll(b, n - d, 0), pos < group - d
        else:
            a_sh, b_sh, keep = pltpu.roll(a, d, 0), pltpu.roll(b, d, 0), pos >= d
        b = jnp.where(keep, a * b_sh + b, b)
        a = jnp.where(keep, a * a_sh, a)
        d *= 2
    return a, b


def _compose_scan_sublanes(a, b, reverse):
    pos = lax.broadcasted_iota(jnp.int32, (1, SUBLANES, 1), 1)
    d = 1
    while d < SUBLANES:
        if reverse:
            a_sh, b_sh, keep = pltpu.roll(a, SUBLANES - d, 1), pltpu.roll(b, SUBLANES - d, 1), pos < SUBLANES - d
        else:
            a_sh, b_sh, keep = pltpu.roll(a, d, 1), pltpu.roll(b, d, 1), pos >= d
        b = jnp.where(keep, a * b_sh + b, b)
        a = jnp.where(keep, a * a_sh, a)
        d *= 2
    return a, b


def _scan_chunk(a, b, h0, reverse, a_sc, b_sc):
    n, c = a.shape
    ng = n // SUBLANES
    a, b = _compose_scan_sublanes(a.reshape(ng, SUBLANES, c), b.reshape(ng, SUBLANES, c), reverse)
    a, b = a.reshape(n, c), b.reshape(n, c)
    edge = 0 if reverse else SUBLANES - 1
    ag, bg = [], []
    for lb in range(c // LANES):
        a_sc[lb, 0:n, :] = a[:, lb * LANES:(lb + 1) * LANES]
        b_sc[lb, 0:n, :] = b[:, lb * LANES:(lb + 1) * LANES]
        ag.append(a_sc[lb, pl.ds(edge, ng, stride=SUBLANES), :])
        bg.append(b_sc[lb, pl.ds(edge, ng, stride=SUBLANES), :])
    ag, bg = jnp.concatenate(ag, axis=1), jnp.concatenate(bg, axis=1)
    ag, bg = _compose_scan(ag, bg, reverse, ng)
    hg = ag * h0 + bg
    grow = lax.broadcasted_iota(jnp.int32, (ng, 1), 0)
    if reverse:
        h_in = jnp.where(grow == ng - 1, h0, pltpu.roll(hg, ng - 1, 0))
        carry = hg[0:1]
    else:
        h_in = jnp.where(grow == 0, h0, pltpu.roll(hg, 1, 0))
        carry = hg[ng - 1:ng]
    h_in = jnp.broadcast_to(h_in[:, None, :], (ng, SUBLANES, c)).reshape(n, c)
    return a * h_in + b, carry


def _rglru_kernel(x_ref, gy_ref, cw_ref, cb_ref, gaw_ref, gab_ref, gxw_ref, gxb_ref, lam_ref, o_ref,
                  s_ref, a_sc, b_sc, *, n_lat, n_ctx, tc):
    n_rows = n_lat + n_ctx
    cw, cb = cw_ref[...], cb_ref[...]
    z = -lam_ref[...]
    softplus = jnp.maximum(z, 0.0) + jnp.log(1.0 + jnp.exp(-jnp.abs(z)))

    def conv(t0, n, seg0, seg1):
        xs = x_ref[pl.ds(t0, n), :]
        prev = x_ref[pl.ds(pl.multiple_of(jnp.maximum(t0 - 8, 0), 8), 8), :]
        nxt = x_ref[pl.ds(pl.multiple_of(jnp.minimum(t0 + n, n_rows - 8), 8), 8), :]
        prev = jnp.where(t0 > seg0, prev, 0.0)
        nxt = jnp.where(t0 + n < seg1, nxt, 0.0)
        ext = jnp.concatenate([prev, xs, nxt], axis=0)
        return (ext[6:6 + n] * cw[0:1] + ext[7:7 + n] * cw[1:2] + ext[8:8 + n] * cw[2:3]
                + ext[9:9 + n] * cw[3:4] + cb)

    for d in range(2):
        reverse = d == 1
        wa = gaw_ref[d, 0].astype(BF16)
        wx = gxw_ref[d, 0].astype(BF16)
        ba, bx = gab_ref[d:d + 1, :], gxb_ref[d:d + 1, :]
        c8 = -RG_C * softplus[d:d + 1, :]

        def chunk(t0, n, seg0, seg1, h0, reverse=reverse, wa=wa, wx=wx, ba=ba, bx=bx, c8=c8):
            xc = conv(t0, n, seg0, seg1)
            xb = xc.astype(BF16)
            r = jax.nn.sigmoid(jnp.dot(xb, wa, preferred_element_type=F32) + ba)
            i = jax.nn.sigmoid(jnp.dot(xb, wx, preferred_element_type=F32) + bx)
            a = jnp.exp(c8 * r)
            b = jnp.sqrt(1.0 - a * a) * (i * xc)
            return _scan_chunk(a, b, h0, reverse, a_sc, b_sc)

        _, h_end = chunk(n_lat, n_ctx, n_lat, n_rows, jnp.zeros((1, cw.shape[1]), F32))
        n_ch = n_lat // tc

        def body(ci, hc, chunk=chunk, reverse=reverse):
            cidx = (n_ch - 1 - ci) if reverse else ci
            t0 = pl.multiple_of(cidx * tc, tc)
            h, hc = chunk(t0, tc, 0, n_lat, hc)
            if reverse:
                s = s_ref[pl.ds(t0, tc), :] + h
                o_ref[pl.ds(t0, tc), :] = (s * gy_ref[pl.ds(t0, tc), :].astype(F32)).astype(o_ref.dtype)
            else:
                s_ref[pl.ds(t0, tc), :] = h
            return hc

        lax.fori_loop(0, n_ch, body, h_end)


def _rglru(xr, gy, conv_w, conv_b, ga_w, ga_b, gx_w, gx_b, lam, n_lat, n_ctx):
    n_rows, c_tot = xr.shape
    nb, bw = ga_w.shape[1], ga_w.shape[2]
    tc = _tile(n_lat, 256, 64)
    assert n_ctx % 64 == 0
    col = lambda j: (0, j)
    return pl.pallas_call(
        functools.partial(_rglru_kernel, n_lat=n_lat, n_ctx=n_ctx, tc=tc),
        grid=(nb,),
        in_specs=[pl.BlockSpec((n_rows, bw), col),
                  pl.BlockSpec((n_lat, bw), col),
                  pl.BlockSpec((conv_w.shape[0], bw), col),
                  pl.BlockSpec((1, bw), col),
                  pl.BlockSpec((2, 1, bw, bw), lambda j: (0, j, 0, 0)),
                  pl.BlockSpec((2, bw), col),
                  pl.BlockSpec((2, 1, bw, bw), lambda j: (0, j, 0, 0)),
                  pl.BlockSpec((2, bw), col),
                  pl.BlockSpec((2, bw), col)],
        out_specs=pl.BlockSpec((n_lat, bw), col),
        out_shape=jax.ShapeDtypeStruct((n_lat, c_tot), BF16),
        scratch_shapes=[pltpu.VMEM((n_lat, bw), F32),
                        pltpu.VMEM((bw // LANES, max(tc, n_ctx), LANES), F32),
                        pltpu.VMEM((bw // LANES, max(tc, n_ctx), LANES), F32)],
        compiler_params=_cparams(("arbitrary",)),
        name="rglru",
    )(xr, gy, conv_w, conv_b.reshape(1, c_tot), ga_w, ga_b, gx_w, gx_b, lam)


def _row_copy(src_hbm, row, dst, r, sem):
    return pltpu.make_async_copy(src_hbm.at[pl.ds(row, 1)], dst.at[pl.ds(r, 1)], sem)


def _gather_kernel(src_ref, nu_ref, h_hbm, o_ref, buf, sem, *, tm):
    t = pl.program_id(0)
    n_used = nu_ref[0]

    def issue(tile, slot):
        def start(r, c):
            _row_copy(h_hbm, src_ref[tile * tm + r], buf.at[slot], r, sem.at[slot]).start()
            return c
        lax.fori_loop(0, tm, start, 0, unroll=8)

    @pl.when(t == 0)
    def _():
        issue(0, 0)

    @pl.when(t + 1 < n_used)
    def _():
        issue(t + 1, (t + 1) % 2)

    @pl.when(t < n_used)
    def _():
        slot = t % 2
        pltpu.make_async_copy(h_hbm.at[pl.ds(0, tm)], buf.at[slot], sem.at[slot]).wait()
        o_ref[...] = buf[slot].astype(o_ref.dtype)

    @pl.when(t >= n_used)
    def _():
        o_ref[...] = jnp.zeros(o_ref.shape, o_ref.dtype)


def _moe_gather(h, src_tok, n_used, n_rows, tm):
    d = h.shape[1]
    return pl.pallas_call(
        functools.partial(_gather_kernel, tm=tm),
        grid_spec=pltpu.PrefetchScalarGridSpec(
            num_scalar_prefetch=2,
            grid=(n_rows // tm,),
            in_specs=[pl.BlockSpec(memory_space=pl.ANY)],
            out_specs=pl.BlockSpec((tm, d), lambda t, src, nu: (t, 0)),
            scratch_shapes=[pltpu.VMEM((2, tm, d), F32), pltpu.SemaphoreType.DMA((2,))]),
        out_shape=jax.ShapeDtypeStruct((n_rows, d), BF16),
        compiler_params=_cparams(("arbitrary",)),
        name="moe_gather",
    )(src_tok, n_used, h)


def _expert_rows_kernel(ts_ref, nt_ref, nu_ref, x_hbm, *refs, n_w, tm, tn, unit, compute):
    w_hbm, o_hbm = refs[:n_w], refs[n_w]
    wf = refs[n_w + 1:2 * n_w + 1]
    wb = refs[2 * n_w + 1:3 * n_w + 1]
    xbuf, obuf, sem_in, sem_out, sem_w = refs[3 * n_w + 1:]
    j, e = pl.program_id(0), pl.program_id(1)
    n_j, n_e = pl.num_programs(0), pl.num_programs(1)
    nt = nt_ref[e]
    row0 = ts_ref[e] * tm
    col0 = pl.multiple_of(j * tn, tn)

    tu = unit * tm
    n_full = nt // unit
    tail = nt - unit * n_full

    def w_copy(jj, ee, i):
        cols = pl.ds(pl.multiple_of(jj * tn, tn), tn)
        return pltpu.make_async_copy(w_hbm[i].at[ee, :, cols], wf[i], sem_w.at[i])

    def in_copy(u, rows, slot):
        r = pl.multiple_of(row0 + u * tu, tm)
        return pltpu.make_async_copy(x_hbm.at[pl.ds(r, rows)], xbuf.at[slot, pl.ds(0, rows)], sem_in.at[slot])

    def out_copy(u, rows, slot):
        r = pl.multiple_of(row0 + u * tu, tm)
        return pltpu.make_async_copy(obuf.at[slot, pl.ds(0, rows)], o_hbm.at[pl.ds(r, rows), pl.ds(col0, tn)],
                                     sem_out.at[slot])

    def run(rows, slot):
        x = xbuf[slot, pl.ds(0, rows), :]
        obuf[slot, pl.ds(0, rows), :] = compute(x, *[w_b[...] for w_b in wb]).astype(obuf.dtype)

    step = j * n_e + e

    @pl.when(step == 0)
    def _():
        for i in range(n_w):
            w_copy(0, 0, i).start()

    @pl.when(n_full > 0)
    def _():
        in_copy(0, tu, 0).start()

    for i in range(n_w):
        w_copy(j, e, i).wait()
        wb[i][...] = wf[i][...].astype(BF16)

    @pl.when(step + 1 < n_j * n_e)
    def _():
        jn = (step + 1) // n_e
        en = step + 1 - jn * n_e
        for i in range(n_w):
            w_copy(jn, en, i).start()

    @pl.when(n_full > 0)
    def _():
        def body(u, c):
            slot = u % 2

            @pl.when(u + 1 < n_full)
            def _():
                in_copy(u + 1, tu, 1 - slot).start()

            in_copy(u, tu, slot).wait()

            @pl.when(u >= 2)
            def _():
                out_copy(u - 2, tu, slot).wait()

            run(tu, slot)
            out_copy(u, tu, slot).start()
            return c

        lax.fori_loop(0, n_full, body, 0)

        @pl.when(n_full >= 2)
        def _():
            out_copy(n_full - 2, tu, n_full % 2).wait()
        out_copy(n_full - 1, tu, (n_full - 1) % 2).wait()

    if unit > 1:
        @pl.when(tail > 0)
        def _():
            cp_in = in_copy(n_full, tm, 0)
            cp_in.start()
            cp_in.wait()
            run(tm, 0)
            cp_out = out_copy(n_full, tm, 0)
            cp_out.start()
            cp_out.wait()

    @pl.when(e == pl.num_programs(1) - 1)
    def _():
        obuf[0] = jnp.zeros(obuf.shape[1:], obuf.dtype)

        def zero_tile(t, c):
            cp = pltpu.make_async_copy(obuf.at[0, pl.ds(0, tm)],
                                       o_hbm.at[pl.ds(pl.multiple_of(t * tm, tm), tm), pl.ds(col0, tn)],
                                       sem_out.at[0])
            cp.start()
            cp.wait()
            return c

        lax.fori_loop(nu_ref[0], o_hbm.shape[0] // tm, zero_tile, 0)


def _swiglu_tile(x, wg, wu):
    g = jnp.dot(x, wg, preferred_element_type=F32)
    u = jnp.dot(x, wu, preferred_element_type=F32)
    return g * jax.nn.sigmoid(g) * u


def _down_tile(x, wd):
    return jnp.dot(x, wd, preferred_element_type=F32)


def _moe_expert_matmul(tile_start, n_tiles, n_used, xs, weights, compute, out_dtype, tm, tn, unit, name):
    n_rows, k = xs.shape
    n_exp, _, n_out = weights[0].shape
    n_w = len(weights)
    any_spec = pl.BlockSpec(memory_space=pl.ANY)
    return pl.pallas_call(
        functools.partial(_expert_rows_kernel, n_w=n_w, tm=tm, tn=tn, unit=unit, compute=compute),
        grid_spec=pltpu.PrefetchScalarGridSpec(
            num_scalar_prefetch=3,
            grid=(n_out // tn, n_exp),
            in_specs=[any_spec] * (1 + n_w),
            out_specs=any_spec,
            scratch_shapes=[pltpu.VMEM((k, tn), F32)] * n_w + [pltpu.VMEM((k, tn), BF16)] * n_w + [
                pltpu.VMEM((2, unit * tm, k), xs.dtype), pltpu.VMEM((2, unit * tm, tn), out_dtype),
                pltpu.SemaphoreType.DMA((2,)), pltpu.SemaphoreType.DMA((2,)),
                pltpu.SemaphoreType.DMA((n_w,))]),
        out_shape=jax.ShapeDtypeStruct((n_rows, n_out), out_dtype),
        compiler_params=_cparams(("arbitrary", "arbitrary"), vmem=VMEM_LIMIT_MOE),
        name=name,
    )(tile_start, n_tiles, n_used, xs, *weights)


def _combine_kernel(dest_ref, x_ref, g_ref, w_ref, y_hbm, o_ref, buf, sem, *, tt):
    i = pl.program_id(0)

    def issue(tile, slot):
        def start(r, c):
            tok = tile * tt + r
            for kk in range(TOP_K):
                _row_copy(y_hbm, dest_ref[TOP_K * tok + kk], buf.at[slot, kk], r, sem.at[slot]).start()
            return c
        lax.fori_loop(0, tt, start, 0, unroll=4)

    @pl.when(i == 0)
    def _():
        issue(0, 0)

    @pl.when(i + 1 < pl.num_programs(0))
    def _():
        issue(i + 1, (i + 1) % 2)

    slot = i % 2

    for kk in range(TOP_K):
        pltpu.make_async_copy(y_hbm.at[pl.ds(0, tt)], buf.at[slot, kk], sem.at[slot]).wait()
    w = w_ref[...]
    mix = w[:, 0:1] * buf[slot, 0] + w[:, 1:2] * buf[slot, 1]
    o_ref[...] = x_ref[...] + g_ref[0:1, :] * mix


def _moe_combine(dest, x, gate, w_top, y, n_tok, tt):
    d = x.shape[1]
    return pl.pallas_call(
        functools.partial(_combine_kernel, tt=tt),
        grid_spec=pltpu.PrefetchScalarGridSpec(
            num_scalar_prefetch=1,
            grid=(n_tok // tt,),
            in_specs=[pl.BlockSpec((tt, d), lambda i, dst: (i, 0)),
                      pl.BlockSpec((2, d), lambda i, dst: (0, 0)),
                      pl.BlockSpec((tt, LANES), lambda i, dst: (i, 0)),
                      pl.BlockSpec(memory_space=pl.ANY)],
            out_specs=pl.BlockSpec((tt, d), lambda i, dst: (i, 0)),
            scratch_shapes=[pltpu.VMEM((2, TOP_K, tt, d), F32), pltpu.SemaphoreType.DMA((2,))]),
        out_shape=jax.ShapeDtypeStruct((n_tok, d), F32),
        compiler_params=_cparams(("arbitrary",)),
        name="moe_combine",
    )(dest, x, gate, w_top, y)


def _moe_plan(e_idx, n_exp, tm):
    n_tok = e_idx.shape[0]
    n_slot = n_tok * TOP_K
    flat_e = e_idx.reshape(n_slot)
    onehot = (flat_e[:, None] == jnp.arange(n_exp, dtype=jnp.int32)[None, :]).astype(jnp.int32)
    csum = jnp.cumsum(onehot, axis=0)
    rank = jnp.sum(csum * onehot, axis=1) - 1
    counts = csum[-1]
    n_tiles = (counts + tm - 1) // tm
    tile_end = jnp.cumsum(n_tiles)
    tile_start = tile_end - n_tiles
    dest = (tile_start[flat_e] * tm + rank).astype(jnp.int32)
    n_rows = (n_slot // tm + n_exp) * tm
    src_tok = jnp.zeros((n_rows,), jnp.int32).at[dest].set(jnp.arange(n_slot, dtype=jnp.int32) // TOP_K)
    return dest, src_tok, tile_start.astype(jnp.int32), n_tiles.astype(jnp.int32), tile_end[-1:].astype(jnp.int32), n_rows


def _rope_tables(n_lat, n_ctx):
    half = HEAD_DIM // 2
    inv = ROPE_THETA ** (-jnp.arange(0, half, 2, dtype=F32) / half)
    rows = n_lat // GRID_W
    row = jnp.repeat(jnp.arange(rows, dtype=F32), GRID_W)
    col = jnp.tile(jnp.arange(GRID_W, dtype=F32), rows)
    ang = jnp.concatenate([row[:, None] * inv, col[:, None] * inv], axis=-1)
    cos = jnp.repeat(jnp.cos(ang), 2, axis=-1)
    sin = jnp.repeat(jnp.sin(ang), 2, axis=-1)
    even = (jnp.arange(HEAD_DIM) % 2 == 0)[None, :]
    sa = jnp.where(even, -sin, 0.0)
    sb = jnp.where(even, 0.0, sin)
    ident = jnp.ones((n_ctx, HEAD_DIM), F32)
    zero = jnp.zeros((n_ctx, HEAD_DIM), F32)
    return (jnp.concatenate([cos, ident]), jnp.concatenate([sa, zero]), jnp.concatenate([sb, zero]))


def kernel(x, c, ctx, c_ctx, attn_w_mod, attn_b_mod, attn_norm_mix, attn_norm_ffn, attn_w_in, attn_gqa_q_norm, attn_gqa_k_norm, attn_diff_q_norm, attn_diff_k_norm, attn_diff_lambda_q1, attn_diff_lambda_k1, attn_diff_lambda_q2, attn_diff_lambda_k2, attn_diff_subln, attn_w_out, ffn_w_gate, ffn_w_up, ffn_w_down, rec_w_mod, rec_b_mod, rec_norm_mix, rec_norm_ffn, rec_w_in, rec_conv_w, rec_conv_b, rec_gate_a_w, rec_gate_a_b, rec_gate_x_w, rec_gate_x_b, rec_lru_lambda, rec_w_out, moe_router_w, moe_router_b, moe_w_gate, moe_w_up, moe_w_down):
    assert x.shape[0] == 1 and attn_w_in.shape[0] == 1 and rec_w_in.shape[0] == 1
    n_lat, d = x.shape[1], x.shape[2]
    n_ctx = ctx.shape[1]
    n_rows = n_lat + n_ctx
    hd = HEAD_DIM
    gqa_heads, diff_heads = d // 256, d // 512
    kv_heads = gqa_heads // 4
    n_rep = gqa_heads // kv_heads
    gq_w, gkv_w = gqa_heads * hd, kv_heads * hd
    dq_w, dv_w = diff_heads * 2 * hd, diff_heads * 2 * hd
    col_ka, col_va, col_qb = gq_w, gq_w + gkv_w, gq_w + 2 * gkv_w
    col_kb, col_vb = col_qb + dq_w, col_qb + 2 * dq_w
    d_ff = ffn_w_gate.shape[2]
    n_exp = moe_router_w.shape[2]

    xs = jnp.concatenate([x[0], ctx[0]], axis=0)
    c2 = jnp.zeros((8, d), F32).at[0].set(c[0]).at[1].set(c_ctx)
    tm_big = _tile(n_rows, 1056, 16)

    mods = _modvec(c2, attn_w_mod[0], attn_b_mod[0])[:2].reshape(2, 6, d)
    sh1, sc1, g1, sh2, sc2, g2 = (mods[:, j] for j in range(6))
    h = _normmod(xs, attn_norm_mix[0], sh1, sc1, n_lat)

    scale = hd ** -0.5 * math.log2(math.e)
    tn_qkv = min(4 * hd, gkv_w)
    gains = jnp.concatenate([
        jnp.tile(attn_gqa_q_norm[0] * scale, gqa_heads), jnp.tile(attn_gqa_k_norm[0], kv_heads),
        jnp.ones((gkv_w,), F32),
        jnp.tile((attn_diff_q_norm[0] * scale).reshape(-1), diff_heads),
        jnp.tile(attn_diff_k_norm[0].reshape(-1), diff_heads), jnp.ones((dv_w,), F32)]).reshape(1, -1)
    v_tiles = tuple(range(col_va // tn_qkv, col_qb // tn_qkv)) + tuple(
        range(col_vb // tn_qkv, (col_vb + dv_w) // tn_qkv))
    cos, sa, sb = _rope_tables(n_lat, n_ctx)
    qkv = _qkv(h, attn_w_in[0], gains, cos, sa, sb, v_tiles, tm_big, tn_qkv)

    tq_ctx = n_ctx
    tq_gqa = _tile(n_lat, 512, 16)
    tq_diff = _tile(n_lat, 512, 16)
    tk_lat = _tile(n_rows, 1536, LANES)
    layer = 0
    lam_init = 0.8 - 0.6 * math.exp(-0.3 * layer)
    lam_params = jnp.zeros((8, hd), F32).at[0].set(attn_diff_lambda_q1[0]).at[1].set(attn_diff_lambda_k1[0]) \
        .at[2].set(attn_diff_lambda_q2[0]).at[3].set(attn_diff_lambda_k2[0])
    subln = attn_diff_subln[0].reshape(1, 2 * hd)
    gqa_args = dict(n_kv=kv_heads, n_rep=n_rep, q_col0=0, k_col0=col_ka, v_col0=col_va)
    diff_args = dict(n_heads=diff_heads, q_col0=col_qb, k_col0=col_kb, v_col0=col_vb, lam_init=lam_init)
    lat_args = dict(n_q=n_lat, q_blk0=0, n_k=n_rows, k_blk0=0, tk=tk_lat)
    ctx_args = dict(n_q=n_ctx, q_blk0=n_lat // tq_ctx, n_k=n_ctx, k_blk0=n_lat // n_ctx, tk=n_ctx, tq=tq_ctx)
    oa = jnp.concatenate([_gqa_attention(qkv, tq=tq_gqa, **lat_args, **gqa_args),
                          _gqa_attention(qkv, **ctx_args, **gqa_args)])
    ob = jnp.concatenate([_diff_attention(qkv, lam_params, subln, tq=tq_diff, **lat_args, **diff_args),
                          _diff_attention(qkv, lam_params, subln, **ctx_args, **diff_args)])

    x1 = _mm([(oa, gq_w), (ob, dv_w)], attn_w_out[0], m=n_rows, tm=tm_big, tn=512, n_out=d, out_dtype=F32,
             resid=xs, gate=g1, n_lat=n_lat, name="attn_out")
    h = _normmod(x1, attn_norm_ffn[0], sh2, sc2, n_lat)
    hff = _gateup(h, ffn_w_gate[0], ffn_w_up[0], tm_big, 256)
    k_half = d_ff // 2
    x2 = x1
    for kb in range(2):
        x2 = _mm([(hff, k_half)], ffn_w_down[0], m=n_rows, tm=tm_big, tn=256, n_out=d, out_dtype=F32,
                 w_row_blk=kb, a_col_blk=[kb], resid=x2, gate=g2, n_lat=n_lat, name="ffn_down")

    mods = _modvec(c2, rec_w_mod[0], rec_b_mod[0])[:2].reshape(2, 6, d)
    sh1, sc1, g1, sh2, sc2, g2 = (mods[:, j] for j in range(6))
    h = _normmod(x2, rec_norm_mix[0], sh1, sc1, n_lat)
    d_rnn = rec_w_out.shape[1]
    gy = _mm([(h, d)], rec_w_in[0], m=n_rows, tm=tm_big, tn=512, n_out=d_rnn, out_dtype=BF16,
             act="gelu", name="rec_in_y")
    xr = _mm([(h, d)], rec_w_in[0], m=n_rows, tm=tm_big, tn=512, n_out=d_rnn, out_dtype=F32,
             w_col0=d_rnn, name="rec_in_x")
    s = _rglru(xr, gy, rec_conv_w[0], rec_conv_b[0], rec_gate_a_w[0], rec_gate_a_b[0],
               rec_gate_x_w[0], rec_gate_x_b[0], rec_lru_lambda[0], n_lat, n_ctx)
    tm_lat = _tile(n_lat, 1024, 16)
    x3 = _mm([(s, d_rnn)], rec_w_out[0], m=n_lat, tm=tm_lat, tn=512, n_out=d, out_dtype=F32,
             resid=x2, gate=g1, n_lat=n_lat, name="rec_out")

    h4, w_top, e_top = _normmod_router(x3, rec_norm_ffn[0], sh2, sc2, moe_router_w[0], moe_router_b[0], n_lat)
    tm_e = 256
    dest, src_tok, tile_start, n_tiles, n_used, n_sorted = _moe_plan(e_top[:, :TOP_K], n_exp, tm_e)
    xg = _moe_gather(h4, src_tok, n_used, n_sorted, tm_e)
    tn_e = _tile(d, 1024, LANES)
    hs = _moe_expert_matmul(tile_start, n_tiles, n_used, xg, [moe_w_gate[0], moe_w_up[0]], _swiglu_tile, BF16,
                            tm_e, tn_e, 1, "moe_gateup")
    ys = _moe_expert_matmul(tile_start, n_tiles, n_used, hs, [moe_w_down[0]], _down_tile, F32, tm_e, tn_e, 2,
                            "moe_down")
    out = _moe_combine(dest, x3, g2, w_top, ys, n_lat, 256)
    return out.reshape(1, n_lat, d)
```

```python
import functools
import math

import jax
import jax.numpy as jnp
from jax import lax
from jax.experimental import pallas as pl
from jax.experimental.pallas import tpu as pltpu

F32 = jnp.float32
BF16 = jnp.bfloat16
EPS = 1e-6
ROPE_THETA = 10000.0
GRID_W = 64
HEAD_DIM = 128
RG_C = 8.0
TOP_K = 2
LANES = 128
SUBLANES = 8
VMEM_LIMIT = 56 * 1024 * 1024
VMEM_LIMIT_MOE = 58 * 1024 * 1024
NT_DIMS = (((1,), (1,)), ((), ()))


def _cparams(sem, vmem=VMEM_LIMIT):
    return pltpu.CompilerParams(dimension_semantics=sem, vmem_limit_bytes=vmem)


def _tile(n, target, mult):
    best = None
    for t in range(mult, min(n, target) + 1, mult):
        if n % t == 0:
            best = t
    assert best is not None, (n, target, mult)
    return best


def _row_select(row0, tm, n_lat, ref):
    row = row0 + lax.broadcasted_iota(jnp.int32, (tm, 1), 0)
    return jnp.where(row < n_lat, ref[0:1, :], ref[1:2, :])


def _modvec_kernel(c_ref, w_ref, b_ref, o_ref):
    c = c_ref[...]
    a = (c * jax.nn.sigmoid(c)).astype(BF16)
    o_ref[...] = jnp.dot(a, w_ref[...].astype(BF16), preferred_element_type=F32) + b_ref[...]


def _modvec(c2, w_mod, b_mod):
    d, n = w_mod.shape
    tn = _tile(n, 512, LANES)
    return pl.pallas_call(
        _modvec_kernel,
        grid=(n // tn,),
        in_specs=[pl.BlockSpec((8, d), lambda j: (0, 0)),
                  pl.BlockSpec((d, tn), lambda j: (0, j)),
                  pl.BlockSpec((1, tn), lambda j: (0, j))],
        out_specs=pl.BlockSpec((8, tn), lambda j: (0, j)),
        out_shape=jax.ShapeDtypeStruct((8, n), F32),
        compiler_params=_cparams(("arbitrary",)),
        name="modvec",
    )(c2, w_mod, b_mod.reshape(1, n))


def _normmod_kernel(x_ref, g_ref, sh_ref, sc_ref, o_ref, *, tm, n_lat):
    x = x_ref[...]
    y = x * lax.rsqrt(jnp.mean(x * x, axis=-1, keepdims=True) + EPS) * g_ref[...]
    row0 = pl.program_id(0) * tm
    sc = _row_select(row0, tm, n_lat, sc_ref)
    sh = _row_select(row0, tm, n_lat, sh_ref)
    o_ref[...] = (y * (1.0 + sc) + sh).astype(o_ref.dtype)


def _normmod(x, g, sh, sc, n_lat, m=None):
    m = x.shape[0] if m is None else m
    d = x.shape[1]
    tm = _tile(m, 528, 16)
    vec = pl.BlockSpec((1, d), lambda i: (0, 0))
    two = pl.BlockSpec((2, d), lambda i: (0, 0))
    return pl.pallas_call(
        functools.partial(_normmod_kernel, tm=tm, n_lat=n_lat),
        grid=(m // tm,),
        in_specs=[pl.BlockSpec((tm, d), lambda i: (i, 0)), vec, two, two],
        out_specs=pl.BlockSpec((tm, d), lambda i: (i, 0)),
        out_shape=jax.ShapeDtypeStruct((m, d), BF16),
        compiler_params=_cparams(("arbitrary",)),
        name="normmod",
    )(x, g.reshape(1, d), sh, sc)


def _normmod_router_kernel(x_ref, g_ref, sh_ref, sc_ref, rw_ref, rb_ref, h_ref, w_ref, e_ref, *, n_exp):
    x = x_ref[...]
    y = x * lax.rsqrt(jnp.mean(x * x, axis=-1, keepdims=True) + EPS) * g_ref[...]
    h = y * (1.0 + sc_ref[0:1, :]) + sh_ref[0:1, :]
    h_ref[...] = h
    logits = jnp.dot(h, rw_ref[...], preferred_element_type=F32,
                     precision=lax.Precision.HIGHEST) + rb_ref[...]
    lane = lax.broadcasted_iota(jnp.int32, logits.shape, 1)
    lanef = lane.astype(F32)
    neg = jnp.float32(-jnp.inf)
    lg = jnp.where(lane < n_exp, logits, neg)
    m1 = jnp.max(lg, axis=-1, keepdims=True)
    i1 = jnp.min(jnp.where(lg == m1, lanef, float(LANES)), axis=-1, keepdims=True)
    lg2 = jnp.where(lanef == i1, neg, lg)
    m2 = jnp.max(lg2, axis=-1, keepdims=True)
    i2 = jnp.min(jnp.where(lg2 == m2, lanef, float(LANES)), axis=-1, keepdims=True)
    e = jnp.exp(m2 - m1)
    w1 = 1.0 / (1.0 + e)
    w2 = e / (1.0 + e)
    w_ref[...] = jnp.where(lane == 0, w1, jnp.where(lane == 1, w2, 0.0))
    e_ref[...] = jnp.where(lane == 0, i1, jnp.where(lane == 1, i2, 0.0)).astype(jnp.int32)


def _normmod_router(x, g, sh, sc, router_w, router_b, m):
    d = x.shape[1]
    n_exp = router_w.shape[1]
    tm = _tile(m, 256, 8)
    rw = jnp.zeros((d, LANES), F32).at[:, :n_exp].set(router_w)
    rb = jnp.zeros((1, LANES), F32).at[0, :n_exp].set(router_b)
    vec = pl.BlockSpec((1, d), lambda i: (0, 0))
    two = pl.BlockSpec((2, d), lambda i: (0, 0))
    return pl.pallas_call(
        functools.partial(_normmod_router_kernel, n_exp=n_exp),
        grid=(m // tm,),
        in_specs=[pl.BlockSpec((tm, d), lambda i: (i, 0)), vec, two, two,
                  pl.BlockSpec((d, LANES), lambda i: (0, 0)),
                  pl.BlockSpec((1, LANES), lambda i: (0, 0))],
        out_specs=[pl.BlockSpec((tm, d), lambda i: (i, 0)),
                   pl.BlockSpec((tm, LANES), lambda i: (i, 0)),
                   pl.BlockSpec((tm, LANES), lambda i: (i, 0))],
        out_shape=[jax.ShapeDtypeStruct((m, d), F32),
                   jax.ShapeDtypeStruct((m, LANES), F32),
                   jax.ShapeDtypeStruct((m, LANES), jnp.int32)],
        compiler_params=_cparams(("arbitrary",)),
        name="normmod_router",
    )(x, g.reshape(1, d), sh, sc, rw, rb)


def _mm_kernel(*refs, ks, act, has_res, tm, n_lat):
    n_a = len(ks)
    a_refs, w_ref = refs[:n_a], refs[n_a]
    o_ref = refs[-1]
    w = w_ref[...].astype(BF16)
    acc, off = None, 0
    for a_ref, k in zip(a_refs, ks):
        part = jnp.dot(a_ref[...], w[off:off + k], preferred_element_type=F32)
        acc = part if acc is None else acc + part
        off += k
    if act == "gelu":
        acc = jax.nn.gelu(acc, approximate=True)
    if has_res:
        x_ref, gate_ref = refs[n_a + 1], refs[n_a + 2]
        gate = _row_select(pl.program_id(0) * tm, tm, n_lat, gate_ref)
        acc = x_ref[...] + gate * acc
    o_ref[...] = acc.astype(o_ref.dtype)


def _mm(a_list, w, *, m, tm, tn, n_out, out_dtype, w_row_blk=0, w_col0=0, a_col_blk=None,
        act=None, resid=None, gate=None, n_lat=0, name="mm"):
    ks = tuple(k for _, k in a_list)
    arrs = [a for a, _ in a_list]
    kt = sum(ks)
    a_col_blk = a_col_blk or [0] * len(arrs)
    assert m % tm == 0 and n_out % tn == 0 and w_col0 % tn == 0
    c0 = w_col0 // tn
    in_specs = [pl.BlockSpec((tm, k), functools.partial(lambda i, j, cb: (i, cb), cb=cb))
                for k, cb in zip(ks, a_col_blk)]
    in_specs.append(pl.BlockSpec((kt, tn), lambda i, j: (w_row_blk, c0 + j)))
    args = arrs + [w]
    if resid is not None:
        in_specs += [pl.BlockSpec((tm, tn), lambda i, j: (i, j)),
                     pl.BlockSpec((2, tn), lambda i, j: (0, j))]
        args += [resid, gate]
    return pl.pallas_call(
        functools.partial(_mm_kernel, ks=ks, act=act, has_res=resid is not None, tm=tm, n_lat=n_lat),
        grid=(m // tm, n_out // tn),
        in_specs=in_specs,
        out_specs=pl.BlockSpec((tm, tn), lambda i, j: (i, j)),
        out_shape=jax.ShapeDtypeStruct((m, n_out), out_dtype),
        compiler_params=_cparams(("arbitrary", "arbitrary")),
        name=name,
    )(*args)


def _qkv_kernel(a_ref, w_ref, g_ref, cos_ref, sa_ref, sb_ref, o_ref, ybuf, *, v_tiles, hd):
    i, j = pl.program_id(0), pl.program_id(1)

    @pl.when(jnp.logical_and(i == 0, j == 0))
    def _():
        ybuf[...] = jnp.zeros(ybuf.shape, ybuf.dtype)

    is_v = functools.reduce(jnp.logical_or, [j - 1 == t for t in v_tiles])

    def step(new, old):
        y = ybuf[old]
        ybuf[new] = jnp.dot(a_ref[...], w_ref[...].astype(BF16), preferred_element_type=F32)
        cos, sa, sb = cos_ref[...], sa_ref[...], sb_ref[...]
        for h in range(y.shape[1] // hd):
            yh = y[:, h * hd:(h + 1) * hd]
            yn = yh * lax.rsqrt(jnp.mean(yh * yh, axis=-1, keepdims=True) + EPS) * g_ref[:, h * hd:(h + 1) * hd]
            out = yn * cos + pltpu.roll(yn, hd - 1, 1) * sa + pltpu.roll(yn, 1, 1) * sb
            o_ref[:, h * hd:(h + 1) * hd] = jnp.where(is_v, yh, out).astype(o_ref.dtype)

    @pl.when(j % 2 == 0)
    def _():
        step(0, 1)

    @pl.when(j % 2 == 1)
    def _():
        step(1, 0)


def _qkv(h, w_in, gains, cos, sa, sb, v_tiles, tm, tn):
    m, d = h.shape
    n = w_in.shape[1]
    nt = n // tn
    tab = pl.BlockSpec((tm, HEAD_DIM), lambda i, j: (i, 0))
    prev = lambda i, j: (0, jnp.maximum(j - 1, 0))
    return pl.pallas_call(
        functools.partial(_qkv_kernel, v_tiles=v_tiles, hd=HEAD_DIM),
        grid=(m // tm, nt + 1),
        in_specs=[pl.BlockSpec((tm, d), lambda i, j: (i, 0)),
                  pl.BlockSpec((d, tn), lambda i, j: (0, jnp.minimum(j, nt - 1))),
                  pl.BlockSpec((1, tn), prev),
                  tab, tab, tab],
        out_specs=pl.BlockSpec((tm, tn), lambda i, j: (i, jnp.maximum(j - 1, 0))),
        out_shape=jax.ShapeDtypeStruct((m, n), BF16),
        scratch_shapes=[pltpu.VMEM((2, tm, tn), F32)],
        compiler_params=_cparams(("arbitrary", "arbitrary")),
        name="qkv_proj",
    )(h, w_in, gains, cos, sa, sb)


def _gateup_kernel(a_ref, wg_ref, wu_ref, o_ref):
    a = a_ref[...]
    g = jnp.dot(a, wg_ref[...].astype(BF16), preferred_element_type=F32)
    u = jnp.dot(a, wu_ref[...].astype(BF16), preferred_element_type=F32)
    o_ref[...] = (g * jax.nn.sigmoid(g) * u).astype(o_ref.dtype)


def _gateup(h, wg, wu, tm, tn):
    m, d = h.shape
    n = wg.shape[1]
    wspec = pl.BlockSpec((d, tn), lambda i, j: (0, j))
    return pl.pallas_call(
        _gateup_kernel,
        grid=(m // tm, n // tn),
        in_specs=[pl.BlockSpec((tm, d), lambda i, j: (i, 0)), wspec, wspec],
        out_specs=pl.BlockSpec((tm, tn), lambda i, j: (i, j)),
        out_shape=jax.ShapeDtypeStruct((m, n), BF16),
        compiler_params=_cparams(("arbitrary", "arbitrary")),
        name="ffn_gateup",
    )(h, wg, wu)


def _online_softmax_step(s, m_sc, rows):
    m_old = m_sc[rows, :]
    m_new = jnp.maximum(m_old, jnp.max(s, axis=-1, keepdims=True))
    m_sc[rows, :] = m_new
    return jnp.exp2(s - m_new), jnp.exp2(m_old - m_new)


CHAIN_HEADS = 1


def _chunk_loop(body, n_chunks):
    lax.fori_loop(0, n_chunks, body, 0, unroll=2 if n_chunks % 2 == 0 else 1)


def _gqa_kernel(q_ref, k_ref, v_ref, o_ref, vaug, m_sc, acc_sc, *, tq, tk, n_chunks, n_rep, hd):
    @pl.when(pl.program_id(1) == 0)
    def _():
        vaug[:, :hd] = v_ref[...]
        vaug[:, hd:] = jnp.ones((vaug.shape[0], hd), vaug.dtype)

    q = jnp.concatenate([q_ref[:, h * hd:(h + 1) * hd] for h in range(n_rep)], axis=0)
    m_sc[...] = jnp.full(m_sc.shape, -jnp.inf, F32)
    acc_sc[...] = jnp.zeros(acc_sc.shape, F32)

    def body(c, carry):
        start = pl.multiple_of(c * tk, tk)
        k = k_ref[pl.ds(start, tk), :]
        v = vaug[pl.ds(start, tk), :]
        cr = min(CHAIN_HEADS, n_rep) * tq
        for r0 in range(0, n_rep * tq, cr):
            rows = pl.ds(r0, cr)
            s = lax.dot_general(q[r0:r0 + cr], k, NT_DIMS, preferred_element_type=F32)
            p, alpha = _online_softmax_step(s, m_sc, rows)
            acc_sc[rows, :] = alpha * acc_sc[rows, :] + jnp.dot(p.astype(BF16), v, preferred_element_type=F32)
        return carry

    _chunk_loop(body, n_chunks)
    acc = acc_sc[...]
    o = acc[:, :hd] / acc[:, hd:]
    for h in range(n_rep):
        o_ref[:, h * hd:(h + 1) * hd] = o[h * tq:(h + 1) * tq].astype(o_ref.dtype)


def _gqa_attention(qkv, *, n_q, q_blk0, n_k, k_blk0, tq, tk, n_kv, n_rep, q_col0, k_col0, v_col0):
    hd = HEAD_DIM
    qw = n_rep * hd
    return pl.pallas_call(
        functools.partial(_gqa_kernel, tq=tq, tk=tk, n_chunks=n_k // tk, n_rep=n_rep, hd=hd),
        grid=(n_kv, n_q // tq),
        in_specs=[pl.BlockSpec((tq, qw), lambda g, i: (q_blk0 + i, q_col0 // qw + g)),
                  pl.BlockSpec((n_k, hd), lambda g, i: (k_blk0, k_col0 // hd + g)),
                  pl.BlockSpec((n_k, hd), lambda g, i: (k_blk0, v_col0 // hd + g))],
        out_specs=pl.BlockSpec((tq, qw), lambda g, i: (i, g)),
        out_shape=jax.ShapeDtypeStruct((n_q, n_kv * qw), BF16),
        scratch_shapes=[pltpu.VMEM((n_k, 2 * hd), BF16),
                        pltpu.VMEM((n_rep * tq, 1), F32),
                        pltpu.VMEM((n_rep * tq, 2 * hd), F32)],
        compiler_params=_cparams(("arbitrary", "arbitrary")),
        name="gqa_attention",
    )(qkv, qkv, qkv)


def _diff_kernel(q_ref, k_ref, v_ref, lam_ref, sub_ref, o_ref, m_sc, l_sc, acc_sc, *, tq, tk, n_chunks, hd,
                 lam_init):
    q1, q2 = q_ref[:, :hd], q_ref[:, hd:]
    m_sc[...] = jnp.full(m_sc.shape, -jnp.inf, F32)
    l_sc[...] = jnp.zeros(l_sc.shape, F32)
    acc_sc[...] = jnp.zeros(acc_sc.shape, F32)

    def body(c, carry):
        start = pl.multiple_of(c * tk, tk)
        k = k_ref[pl.ds(start, tk), :]
        v = v_ref[pl.ds(start, tk), :]
        s = jnp.concatenate([lax.dot_general(q1, k[:, :hd], NT_DIMS, preferred_element_type=F32),
                             lax.dot_general(q2, k[:, hd:], NT_DIMS, preferred_element_type=F32)], axis=0)
        rows = pl.ds(0, 2 * tq)
        p, alpha = _online_softmax_step(s, m_sc, rows)
        l_sc[...] = alpha * l_sc[...] + jnp.sum(p, axis=-1, keepdims=True)
        acc_sc[...] = alpha * acc_sc[...] + jnp.dot(p.astype(BF16), v, preferred_element_type=F32)
        return carry

    _chunk_loop(body, n_chunks)
    o = acc_sc[...] / l_sc[...]
    lp = lam_ref[...]
    lam = (jnp.exp(jnp.sum(lp[0:1] * lp[1:2], axis=-1, keepdims=True))
           - jnp.exp(jnp.sum(lp[2:3] * lp[3:4], axis=-1, keepdims=True)) + lam_init)
    o = o[:tq] - lam * o[tq:]
    o = o * lax.rsqrt(jnp.mean(o * o, axis=-1, keepdims=True) + EPS) * sub_ref[...] * (1.0 - lam_init)
    o_ref[...] = o.astype(o_ref.dtype)


def _diff_attention(qkv, lam_params, subln, *, n_q, q_blk0, n_k, k_blk0, tq, tk, n_heads,
                    q_col0, k_col0, v_col0, lam_init):
    hd = HEAD_DIM
    w2 = 2 * hd
    return pl.pallas_call(
        functools.partial(_diff_kernel, tq=tq, tk=tk, n_chunks=n_k // tk, hd=hd, lam_init=lam_init),
        grid=(n_heads, n_q // tq),
        in_specs=[pl.BlockSpec((tq, w2), lambda h, i: (q_blk0 + i, q_col0 // w2 + h)),
                  pl.BlockSpec((n_k, w2), lambda h, i: (k_blk0, k_col0 // w2 + h)),
                  pl.BlockSpec((n_k, w2), lambda h, i: (k_blk0, v_col0 // w2 + h)),
                  pl.BlockSpec((8, hd), lambda h, i: (0, 0)),
                  pl.BlockSpec((1, w2), lambda h, i: (0, 0))],
        out_specs=pl.BlockSpec((tq, w2), lambda h, i: (i, h)),
        out_shape=jax.ShapeDtypeStruct((n_q, n_heads * w2), BF16),
        scratch_shapes=[pltpu.VMEM((2 * tq, 1), F32), pltpu.VMEM((2 * tq, 1), F32),
                        pltpu.VMEM((2 * tq, w2), F32)],
        compiler_params=_cparams(("arbitrary", "arbitrary")),
        name="diff_attention",
    )(qkv, qkv, qkv, lam_params, subln)


def _compose_scan(a, b, reverse, group):
    n = a.shape[0]
    pos = lax.broadcasted_iota(jnp.int32, (n, 1), 0) & (group - 1)
    d = 1
    while d < group:
        if reverse:
            a_sh, b_sh, keep = pltpu.roll(a, n - d, 0), pltpu.roll(b, n - d, 0), pos < group - d
        else:
            a_sh, b_sh, keep = pltpu.roll(a, d, 0), pltpu.roll(b, d, 0), pos >= d
        b = jnp.where(keep, a * b_sh + b, b)
        a = jnp.where(keep, a * a_sh, a)
        d *= 2
    return a, b


def _compose_scan_sublanes(a, b, reverse):
    pos = lax.broadcasted_iota(jnp.int32, (1, SUBLANES, 1), 1)
    d = 1
    while d < SUBLANES:
        if reverse:
            a_sh, b_sh, keep = pltpu.roll(a, SUBLANES - d, 1), pltpu.roll(b, SUBLANES - d, 1), pos < SUBLANES - d
        else:
            a_sh, b_sh, keep = pltpu.roll(a, d, 1), pltpu.roll(b, d, 1), pos >= d
        b = jnp.where(keep, a * b_sh + b, b)
        a = jnp.where(keep, a * a_sh, a)
        d *= 2
    return a, b


def _scan_chunk(a, b, h0, reverse, a_sc, b_sc):
    n, c = a.shape
    ng = n // SUBLANES
    a, b = _compose_scan_sublanes(a.reshape(ng, SUBLANES, c), b.reshape(ng, SUBLANES, c), reverse)
    a, b = a.reshape(n, c), b.reshape(n, c)
    edge = 0 if reverse else SUBLANES - 1
    ag, bg = [], []
    for lb in range(c // LANES):
        a_sc[lb, 0:n, :] = a[:, lb * LANES:(lb + 1) * LANES]
        b_sc[lb, 0:n, :] = b[:, lb * LANES:(lb + 1) * LANES]
        ag.append(a_sc[lb, pl.ds(edge, ng, stride=SUBLANES), :])
        bg.append(b_sc[lb, pl.ds(edge, ng, stride=SUBLANES), :])
    ag, bg = jnp.concatenate(ag, axis=1), jnp.concatenate(bg, axis=1)
    ag, bg = _compose_scan(ag, bg, reverse, ng)
    hg = ag * h0 + bg
    grow = lax.broadcasted_iota(jnp.int32, (ng, 1), 0)
    if reverse:
        h_in = jnp.where(grow == ng - 1, h0, pltpu.roll(hg, ng - 1, 0))
        carry = hg[0:1]
    else:
        h_in = jnp.where(grow == 0, h0, pltpu.roll(hg, 1, 0))
        carry = hg[ng - 1:ng]
    h_in = jnp.broadcast_to(h_in[:, None, :], (ng, SUBLANES, c)).reshape(n, c)
    return a * h_in + b, carry


def _rglru_kernel(x_ref, gy_ref, cw_ref, cb_ref, gaw_ref, gab_ref, gxw_ref, gxb_ref, lam_ref, o_ref,
                  s_ref, a_sc, b_sc, *, n_lat, n_ctx, tc):
    n_rows = n_lat + n_ctx
    cw, cb = cw_ref[...], cb_ref[...]
    z = -lam_ref[...]
    softplus = jnp.maximum(z, 0.0) + jnp.log(1.0 + jnp.exp(-jnp.abs(z)))

    def conv(t0, n, seg0, seg1):
        xs = x_ref[pl.ds(t0, n), :]
        prev = x_ref[pl.ds(pl.multiple_of(jnp.maximum(t0 - 8, 0), 8), 8), :]
        nxt = x_ref[pl.ds(pl.multiple_of(jnp.minimum(t0 + n, n_rows - 8), 8), 8), :]
        prev = jnp.where(t0 > seg0, prev, 0.0)
        nxt = jnp.where(t0 + n < seg1, nxt, 0.0)
        ext = jnp.concatenate([prev, xs, nxt], axis=0)
        return (ext[6:6 + n] * cw[0:1] + ext[7:7 + n] * cw[1:2] + ext[8:8 + n] * cw[2:3]
                + ext[9:9 + n] * cw[3:4] + cb)

    for d in range(2):
        reverse = d == 1
        wa = gaw_ref[d, 0].astype(BF16)
        wx = gxw_ref[d, 0].astype(BF16)
        ba, bx = gab_ref[d:d + 1, :], gxb_ref[d:d + 1, :]
        c8 = -RG_C * softplus[d:d + 1, :]

        def chunk(t0, n, seg0, seg1, h0, reverse=reverse, wa=wa, wx=wx, ba=ba, bx=bx, c8=c8):
            xc = conv(t0, n, seg0, seg1)
            xb = xc.astype(BF16)
            r = jax.nn.sigmoid(jnp.dot(xb, wa, preferred_element_type=F32) + ba)
            i = jax.nn.sigmoid(jnp.dot(xb, wx, preferred_element_type=F32) + bx)
            a = jnp.exp(c8 * r)
            b = jnp.sqrt(1.0 - a * a) * (i * xc)
            return _scan_chunk(a, b, h0, reverse, a_sc, b_sc)

        _, h_end = chunk(n_lat, n_ctx, n_lat, n_rows, jnp.zeros((1, cw.shape[1]), F32))
        n_ch = n_lat // tc

        def body(ci, hc, chunk=chunk, reverse=reverse):
            cidx = (n_ch - 1 - ci) if reverse else ci
            t0 = pl.multiple_of(cidx * tc, tc)
            h, hc = chunk(t0, tc, 0, n_lat, hc)
            if reverse:
                s = s_ref[pl.ds(t0, tc), :] + h
                o_ref[pl.ds(t0, tc), :] = (s * gy_ref[pl.ds(t0, tc), :].astype(F32)).astype(o_ref.dtype)
            else:
                s_ref[pl.ds(t0, tc), :] = h
            return hc

        lax.fori_loop(0, n_ch, body, h_end)


def _rglru(xr, gy, conv_w, conv_b, ga_w, ga_b, gx_w, gx_b, lam, n_lat, n_ctx):
    n_rows, c_tot = xr.shape
    nb, bw = ga_w.shape[1], ga_w.shape[2]
    tc = _tile(n_lat, 256, 64)
    assert n_ctx % 64 == 0
    col = lambda j: (0, j)
    return pl.pallas_call(
        functools.partial(_rglru_kernel, n_lat=n_lat, n_ctx=n_ctx, tc=tc),
        grid=(nb,),
        in_specs=[pl.BlockSpec((n_rows, bw), col),
                  pl.BlockSpec((n_lat, bw), col),
                  pl.BlockSpec((conv_w.shape[0], bw), col),
                  pl.BlockSpec((1, bw), col),
                  pl.BlockSpec((2, 1, bw, bw), lambda j: (0, j, 0, 0)),
                  pl.BlockSpec((2, bw), col),
                  pl.BlockSpec((2, 1, bw, bw), lambda j: (0, j, 0, 0)),
                  pl.BlockSpec((2, bw), col),
                  pl.BlockSpec((2, bw), col)],
        out_specs=pl.BlockSpec((n_lat, bw), col),
        out_shape=jax.ShapeDtypeStruct((n_lat, c_tot), BF16),
        scratch_shapes=[pltpu.VMEM((n_lat, bw), F32),
                        pltpu.VMEM((bw // LANES, max(tc, n_ctx), LANES), F32),
                        pltpu.VMEM((bw // LANES, max(tc, n_ctx), LANES), F32)],
        compiler_params=_cparams(("arbitrary",)),
        name="rglru",
    )(xr, gy, conv_w, conv_b.reshape(1, c_tot), ga_w, ga_b, gx_w, gx_b, lam)


def _row_copy(src_hbm, row, dst, r, sem):
    return pltpu.make_async_copy(src_hbm.at[pl.ds(row, 1)], dst.at[pl.ds(r, 1)], sem)


def _gather_kernel(src_ref, nu_ref, h_hbm, o_ref, buf, sem, *, tm):
    t = pl.program_id(0)
    n_used = nu_ref[0]

    def issue(tile, slot):
        def start(r, c):
            _row_copy(h_hbm, src_ref[tile * tm + r], buf.at[slot], r, sem.at[slot]).start()
            return c
        lax.fori_loop(0, tm, start, 0, unroll=8)

    @pl.when(t == 0)
    def _():
        issue(0, 0)

    @pl.when(t + 1 < n_used)
    def _():
        issue(t + 1, (t + 1) % 2)

    @pl.when(t < n_used)
    def _():
        slot = t % 2
        pltpu.make_async_copy(h_hbm.at[pl.ds(0, tm)], buf.at[slot], sem.at[slot]).wait()
        o_ref[...] = buf[slot].astype(o_ref.dtype)

    @pl.when(t >= n_used)
    def _():
        o_ref[...] = jnp.zeros(o_ref.shape, o_ref.dtype)


def _moe_gather(h, src_tok, n_used, n_rows, tm):
    d = h.shape[1]
    return pl.pallas_call(
        functools.partial(_gather_kernel, tm=tm),
        grid_spec=pltpu.PrefetchScalarGridSpec(
            num_scalar_prefetch=2,
            grid=(n_rows // tm,),
            in_specs=[pl.BlockSpec(memory_space=pl.ANY)],
            out_specs=pl.BlockSpec((tm, d), lambda t, src, nu: (t, 0)),
            scratch_shapes=[pltpu.VMEM((2, tm, d), F32), pltpu.SemaphoreType.DMA((2,))]),
        out_shape=jax.ShapeDtypeStruct((n_rows, d), BF16),
        compiler_params=_cparams(("arbitrary",)),
        name="moe_gather",
    )(src_tok, n_used, h)


def _expert_rows_kernel(ts_ref, nt_ref, nu_ref, x_hbm, *refs, n_w, tm, tn, unit, compute):
    w_hbm, o_hbm = refs[:n_w], refs[n_w]
    wf = refs[n_w + 1:2 * n_w + 1]
    wb = refs[2 * n_w + 1:3 * n_w + 1]
    xbuf, obuf, sem_in, sem_out, sem_w = refs[3 * n_w + 1:]
    j, e = pl.program_id(0), pl.program_id(1)
    n_j, n_e = pl.num_programs(0), pl.num_programs(1)
    nt = nt_ref[e]
    row0 = ts_ref[e] * tm
    col0 = pl.multiple_of(j * tn, tn)

    tu = unit * tm
    n_full = nt // unit
    tail = nt - unit * n_full

    def w_copy(jj, ee, i):
        cols = pl.ds(pl.multiple_of(jj * tn, tn), tn)
        return pltpu.make_async_copy(w_hbm[i].at[ee, :, cols], wf[i], sem_w.at[i])

    def in_copy(u, rows, slot):
        r = pl.multiple_of(row0 + u * tu, tm)
        return pltpu.make_async_copy(x_hbm.at[pl.ds(r, rows)], xbuf.at[slot, pl.ds(0, rows)], sem_in.at[slot])

    def out_copy(u, rows, slot):
        r = pl.multiple_of(row0 + u * tu, tm)
        return pltpu.make_async_copy(obuf.at[slot, pl.ds(0, rows)], o_hbm.at[pl.ds(r, rows), pl.ds(col0, tn)],
                                     sem_out.at[slot])

    def run(rows, slot):
        x = xbuf[slot, pl.ds(0, rows), :]
        obuf[slot, pl.ds(0, rows), :] = compute(x, *[w_b[...] for w_b in wb]).astype(obuf.dtype)

    step = j * n_e + e

    @pl.when(step == 0)
    def _():
        for i in range(n_w):
            w_copy(0, 0, i).start()

    @pl.when(n_full > 0)
    def _():
        in_copy(0, tu, 0).start()

    for i in range(n_w):
        w_copy(j, e, i).wait()
        wb[i][...] = wf[i][...].astype(BF16)

    @pl.when(step + 1 < n_j * n_e)
    def _():
        jn = (step + 1) // n_e
        en = step + 1 - jn * n_e
        for i in range(n_w):
            w_copy(jn, en, i).start()

    @pl.when(n_full > 0)
    def _():
        def body(u, c):
            slot = u % 2

            @pl.when(u + 1 < n_full)
            def _():
                in_copy(u + 1, tu, 1 - slot).start()

            in_copy(u, tu, slot).wait()

            @pl.when(u >= 2)
            def _():
                out_copy(u - 2, tu, slot).wait()

            run(tu, slot)
            out_copy(u, tu, slot).start()
            return c

        lax.fori_loop(0, n_full, body, 0)

        @pl.when(n_full >= 2)
        def _():
            out_copy(n_full - 2, tu, n_full % 2).wait()
        out_copy(n_full - 1, tu, (n_full - 1) % 2).wait()

    if unit > 1:
        @pl.when(tail > 0)
        def _():
            cp_in = in_copy(n_full, tm, 0)
            cp_in.start()
            cp_in.wait()
            run(tm, 0)
            cp_out = out_copy(n_full, tm, 0)
            cp_out.start()
            cp_out.wait()

    @pl.when(e == pl.num_programs(1) - 1)
    def _():
        obuf[0] = jnp.zeros(obuf.shape[1:], obuf.dtype)

        def zero_tile(t, c):
            cp = pltpu.make_async_copy(obuf.at[0, pl.ds(0, tm)],
                                       o_hbm.at[pl.ds(pl.multiple_of(t * tm, tm), tm), pl.ds(col0, tn)],
                                       sem_out.at[0])
            cp.start()
            cp.wait()
            return c

        lax.fori_loop(nu_ref[0], o_hbm.shape[0] // tm, zero_tile, 0)


def _swiglu_tile(x, wg, wu):
    g = jnp.dot(x, wg, preferred_element_type=F32)
    u = jnp.dot(x, wu, preferred_element_type=F32)
    return g * jax.nn.sigmoid(g) * u


def _down_tile(x, wd):
    return jnp.dot(x, wd, preferred_element_type=F32)


def _moe_expert_matmul(tile_start, n_tiles, n_used, xs, weights, compute, out_dtype, tm, tn, unit, name):
    n_rows, k = xs.shape
    n_exp, _, n_out = weights[0].shape
    n_w = len(weights)
    any_spec = pl.BlockSpec(memory_space=pl.ANY)
    return pl.pallas_call(
        functools.partial(_expert_rows_kernel, n_w=n_w, tm=tm, tn=tn, unit=unit, compute=compute),
        grid_spec=pltpu.PrefetchScalarGridSpec(
            num_scalar_prefetch=3,
            grid=(n_out // tn, n_exp),
            in_specs=[any_spec] * (1 + n_w),
            out_specs=any_spec,
            scratch_shapes=[pltpu.VMEM((k, tn), F32)] * n_w + [pltpu.VMEM((k, tn), BF16)] * n_w + [
                pltpu.VMEM((2, unit * tm, k), xs.dtype), pltpu.VMEM((2, unit * tm, tn), out_dtype),
                pltpu.SemaphoreType.DMA((2,)), pltpu.SemaphoreType.DMA((2,)),
                pltpu.SemaphoreType.DMA((n_w,))]),
        out_shape=jax.ShapeDtypeStruct((n_rows, n_out), out_dtype),
        compiler_params=_cparams(("arbitrary", "arbitrary"), vmem=VMEM_LIMIT_MOE),
        name=name,
    )(tile_start, n_tiles, n_used, xs, *weights)


def _combine_kernel(dest_ref, x_ref, g_ref, w_ref, y_hbm, o_ref, buf, sem, *, tt):
    i = pl.program_id(0)

    def issue(tile, slot):
        def start(r, c):
            tok = tile * tt + r
            for kk in range(TOP_K):
                _row_copy(y_hbm, dest_ref[TOP_K * tok + kk], buf.at[slot, kk], r, sem.at[slot]).start()
            return c
        lax.fori_loop(0, tt, start, 0, unroll=4)

    @pl.when(i == 0)
    def _():
        issue(0, 0)

    @pl.when(i + 1 < pl.num_programs(0))
    def _():
        issue(i + 1, (i + 1) % 2)

    slot = i % 2

    for kk in range(TOP_K):
        pltpu.make_async_copy(y_hbm.at[pl.ds(0, tt)], buf.at[slot, kk], sem.at[slot]).wait()
    w = w_ref[...]
    mix = w[:, 0:1] * buf[slot, 0] + w[:, 1:2] * buf[slot, 1]
    o_ref[...] = x_ref[...] + g_ref[0:1, :] * mix


def _moe_combine(dest, x, gate, w_top, y, n_tok, tt):
    d = x.shape[1]
    return pl.pallas_call(
        functools.partial(_combine_kernel, tt=tt),
        grid_spec=pltpu.PrefetchScalarGridSpec(
            num_scalar_prefetch=1,
            grid=(n_tok // tt,),
            in_specs=[pl.BlockSpec((tt, d), lambda i, dst: (i, 0)),
                      pl.BlockSpec((2, d), lambda i, dst: (0, 0)),
                      pl.BlockSpec((tt, LANES), lambda i, dst: (i, 0)),
                      pl.BlockSpec(memory_space=pl.ANY)],
            out_specs=pl.BlockSpec((tt, d), lambda i, dst: (i, 0)),
            scratch_shapes=[pltpu.VMEM((2, TOP_K, tt, d), F32), pltpu.SemaphoreType.DMA((2,))]),
        out_shape=jax.ShapeDtypeStruct((n_tok, d), F32),
        compiler_params=_cparams(("arbitrary",)),
        name="moe_combine",
    )(dest, x, gate, w_top, y)


def _moe_plan(e_idx, n_exp, tm):
    n_tok = e_idx.shape[0]
    n_slot = n_tok * TOP_K
    flat_e = e_idx.reshape(n_slot)
    onehot = (flat_e[:, None] == jnp.arange(n_exp, dtype=jnp.int32)[None, :]).astype(jnp.int32)
    csum = jnp.cumsum(onehot, axis=0)
    rank = jnp.sum(csum * onehot, axis=1) - 1
    counts = csum[-1]
    n_tiles = (counts + tm - 1) // tm
    tile_end = jnp.cumsum(n_tiles)
    tile_start = tile_end - n_tiles
    dest = (tile_start[flat_e] * tm + rank).astype(jnp.int32)
    n_rows = (n_slot // tm + n_exp) * tm
    src_tok = jnp.zeros((n_rows,), jnp.int32).at[dest].set(jnp.arange(n_slot, dtype=jnp.int32) // TOP_K)
    return dest, src_tok, tile_start.astype(jnp.int32), n_tiles.astype(jnp.int32), tile_end[-1:].astype(jnp.int32), n_rows


def _rope_tables(n_lat, n_ctx):
    half = HEAD_DIM // 2
    inv = ROPE_THETA ** (-jnp.arange(0, half, 2, dtype=F32) / half)
    rows = n_lat // GRID_W
    row = jnp.repeat(jnp.arange(rows, dtype=F32), GRID_W)
    col = jnp.tile(jnp.arange(GRID_W, dtype=F32), rows)
    ang = jnp.concatenate([row[:, None] * inv, col[:, None] * inv], axis=-1)
    cos = jnp.repeat(jnp.cos(ang), 2, axis=-1)
    sin = jnp.repeat(jnp.sin(ang), 2, axis=-1)
    even = (jnp.arange(HEAD_DIM) % 2 == 0)[None, :]
    sa = jnp.where(even, -sin, 0.0)
    sb = jnp.where(even, 0.0, sin)
    ident = jnp.ones((n_ctx, HEAD_DIM), F32)
    zero = jnp.zeros((n_ctx, HEAD_DIM), F32)
    return (jnp.concatenate([cos, ident]), jnp.concatenate([sa, zero]), jnp.concatenate([sb, zero]))


def kernel(x, c, ctx, c_ctx, attn_w_mod, attn_b_mod, attn_norm_mix, attn_norm_ffn, attn_w_in, attn_gqa_q_norm, attn_gqa_k_norm, attn_diff_q_norm, attn_diff_k_norm, attn_diff_lambda_q1, attn_diff_lambda_k1, attn_diff_lambda_q2, attn_diff_lambda_k2, attn_diff_subln, attn_w_out, ffn_w_gate, ffn_w_up, ffn_w_down, rec_w_mod, rec_b_mod, rec_norm_mix, rec_norm_ffn, rec_w_in, rec_conv_w, rec_conv_b, rec_gate_a_w, rec_gate_a_b, rec_gate_x_w, rec_gate_x_b, rec_lru_lambda, rec_w_out, moe_router_w, moe_router_b, moe_w_gate, moe_w_up, moe_w_down):
    assert x.shape[0] == 1 and attn_w_in.shape[0] == 1 and rec_w_in.shape[0] == 1
    n_lat, d = x.shape[1], x.shape[2]
    n_ctx = ctx.shape[1]
    n_rows = n_lat + n_ctx
    hd = HEAD_DIM
    gqa_heads, diff_heads = d // 256, d // 512
    kv_heads = gqa_heads // 4
    n_rep = gqa_heads // kv_heads
    gq_w, gkv_w = gqa_heads * hd, kv_heads * hd
    dq_w, dv_w = diff_heads * 2 * hd, diff_heads * 2 * hd
    col_ka, col_va, col_qb = gq_w, gq_w + gkv_w, gq_w + 2 * gkv_w
    col_kb, col_vb = col_qb + dq_w, col_qb + 2 * dq_w
    d_ff = ffn_w_gate.shape[2]
    n_exp = moe_router_w.shape[2]

    xs = jnp.concatenate([x[0], ctx[0]], axis=0)
    c2 = jnp.zeros((8, d), F32).at[0].set(c[0]).at[1].set(c_ctx)
    tm_big = _tile(n_rows, 1056, 16)

    mods = _modvec(c2, attn_w_mod[0], attn_b_mod[0])[:2].reshape(2, 6, d)
    sh1, sc1, g1, sh2, sc2, g2 = (mods[:, j] for j in range(6))
    h = _normmod(xs, attn_norm_mix[0], sh1, sc1, n_lat)

    scale = hd ** -0.5 * math.log2(math.e)
    tn_qkv = min(4 * hd, gkv_w)
    gains = jnp.concatenate([
        jnp.tile(attn_gqa_q_norm[0] * scale, gqa_heads), jnp.tile(attn_gqa_k_norm[0], kv_heads),
        jnp.ones((gkv_w,), F32),
        jnp.tile((attn_diff_q_norm[0] * scale).reshape(-1), diff_heads),
        jnp.tile(attn_diff_k_norm[0].reshape(-1), diff_heads), jnp.ones((dv_w,), F32)]).reshape(1, -1)
    v_tiles = tuple(range(col_va // tn_qkv, col_qb // tn_qkv)) + tuple(
        range(col_vb // tn_qkv, (col_vb + dv_w) // tn_qkv))
    cos, sa, sb = _rope_tables(n_lat, n_ctx)
    qkv = _qkv(h, attn_w_in[0], gains, cos, sa, sb, v_tiles, tm_big, tn_qkv)

    tq_ctx = n_ctx
    tq_gqa = _tile(n_lat, 512, 16)
    tq_diff = _tile(n_lat, 512, 16)
    tk_lat = _tile(n_rows, 1536, LANES)
    layer = 0
    lam_init = 0.8 - 0.6 * math.exp(-0.3 * layer)
    lam_params = jnp.zeros((8, hd), F32).at[0].set(attn_diff_lambda_q1[0]).at[1].set(attn_diff_lambda_k1[0]) \
        .at[2].set(attn_diff_lambda_q2[0]).at[3].set(attn_diff_lambda_k2[0])
    subln = attn_diff_subln[0].reshape(1, 2 * hd)
    gqa_args = dict(n_kv=kv_heads, n_rep=n_rep, q_col0=0, k_col0=col_ka, v_col0=col_va)
    diff_args = dict(n_heads=diff_heads, q_col0=col_qb, k_col0=col_kb, v_col0=col_vb, lam_init=lam_init)
    lat_args = dict(n_q=n_lat, q_blk0=0, n_k=n_rows, k_blk0=0, tk=tk_lat)
    ctx_args = dict(n_q=n_ctx, q_blk0=n_lat // tq_ctx, n_k=n_ctx, k_blk0=n_lat // n_ctx, tk=n_ctx, tq=tq_ctx)
    oa = jnp.concatenate([_gqa_attention(qkv, tq=tq_gqa, **lat_args, **gqa_args),
                          _gqa_attention(qkv, **ctx_args, **gqa_args)])
    ob = jnp.concatenate([_diff_attention(qkv, lam_params, subln, tq=tq_diff, **lat_args, **diff_args),
                          _diff_attention(qkv, lam_params, subln, **ctx_args, **diff_args)])

    x1 = _mm([(oa, gq_w), (ob, dv_w)], attn_w_out[0], m=n_rows, tm=tm_big, tn=512, n_out=d, out_dtype=F32,
             resid=xs, gate=g1, n_lat=n_lat, name="attn_out")
    h = _normmod(x1, attn_norm_ffn[0], sh2, sc2, n_lat)
    hff = _gateup(h, ffn_w_gate[0], ffn_w_up[0], tm_big, 256)
    k_half = d_ff // 2
    x2 = x1
    for kb in range(2):
        x2 = _mm([(hff, k_half)], ffn_w_down[0], m=n_rows, tm=tm_big, tn=256, n_out=d, out_dtype=F32,
                 w_row_blk=kb, a_col_blk=[kb], resid=x2, gate=g2, n_lat=n_lat, name="ffn_down")

    mods = _modvec(c2, rec_w_mod[0], rec_b_mod[0])[:2].reshape(2, 6, d)
    sh1, sc1, g1, sh2, sc2, g2 = (mods[:, j] for j in range(6))
    h = _normmod(x2, rec_norm_mix[0], sh1, sc1, n_lat)
    d_rnn = rec_w_out.shape[1]
    gy = _mm([(h, d)], rec_w_in[0], m=n_rows, tm=tm_big, tn=512, n_out=d_rnn, out_dtype=BF16,
             act="gelu", name="rec_in_y")
    xr = _mm([(h, d)], rec_w_in[0], m=n_rows, tm=tm_big, tn=512, n_out=d_rnn, out_dtype=F32,
             w_col0=d_rnn, name="rec_in_x")
    s = _rglru(xr, gy, rec_conv_w[0], rec_conv_b[0], rec_gate_a_w[0], rec_gate_a_b[0],
               rec_gate_x_w[0], rec_gate_x_b[0], rec_lru_lambda[0], n_lat, n_ctx)
    tm_lat = _tile(n_lat, 1024, 16)
    x3 = _mm([(s, d_rnn)], rec_w_out[0], m=n_lat, tm=tm_lat, tn=512, n_out=d, out_dtype=F32,
             resid=x2, gate=g1, n_lat=n_lat, name="rec_out")

    h4, w_top, e_top = _normmod_router(x3, rec_norm_ffn[0], sh2, sc2, moe_router_w[0], moe_router_b[0], n_lat)
    tm_e = 256
    dest, src_tok, tile_start, n_tiles, n_used, n_sorted = _moe_plan(e_top[:, :TOP_K], n_exp, tm_e)
    xg = _moe_gather(h4, src_tok, n_used, n_sorted, tm_e)
    tn_e = _tile(d, 1024, LANES)
    hs = _moe_expert_matmul(tile_start, n_tiles, n_used, xg, [moe_w_gate[0], moe_w_up[0]], _swiglu_tile, BF16,
                            tm_e, tn_e, 1, "moe_gateup")
    ys = _moe_expert_matmul(tile_start, n_tiles, n_used, hs, [moe_w_down[0]], _down_tile, F32, tm_e,
                            _tile(d, 2 * tn_e, LANES), 1, "moe_down")
    out = _moe_combine(dest, x3, g2, w_top, ys, n_lat, 256)
    return out.reshape(1, n_lat, d)
```
